```python
import jax, jax.numpy as jnp
from jax import lax
import numpy as np

D_MODEL = 1024
BATCH = 8
SEQ = 2048
DEPTH = 2

GRID_W = 64
CTX_LEN = 256
EPS = 1e-6

HEAD_DIM = 64
N_HEADS = D_MODEL // (2 * HEAD_DIM)
N_KV_HEADS = max(1, N_HEADS // 4)
Q_PER_KV = N_HEADS // N_KV_HEADS
ATTN_W = N_HEADS * HEAD_DIM
KV_W = N_KV_HEADS * HEAD_DIM
Q_BLOCK = 128
ROPE_THETA = 10000.0
ROPE_PAIRS_PER_AXIS = HEAD_DIM // 4
ATTN_SCALE = HEAD_DIM ** -0.5

CHUNK = 128
SGU_W = D_MODEL // 4
SGU_GROUPS = 4
SGU_GROUP_W = SGU_W // SGU_GROUPS

POOL_WINDOWS = (2, 4, 8, 16)
POOL_W = D_MODEL // 4
POOL_GROUPS = len(POOL_WINDOWS)
POOL_GROUP_W = POOL_W // POOL_GROUPS

MIX_W = ATTN_W + SGU_W + POOL_W
O_Q = 0
O_K = O_Q + ATTN_W
O_V = O_K + KV_W
O_G = O_V + KV_W
O_P = O_G + 2 * SGU_W
IN_W = O_P + POOL_W

D_FF = -(-8 * D_MODEL // (3 * 256)) * 256

kernel_name = "hybrid_parallel_attn_sgu_pool_dit"


def rms_norm(x, g):
    xf = x.astype(jnp.float32)
    y = xf * lax.rsqrt(jnp.mean(xf * xf, axis=-1, keepdims=True) + EPS)
    return (y * g.astype(jnp.float32)).astype(x.dtype)


def modulate(x, g, shift, scale):
    return rms_norm(x, g) * (1 + scale[:, None, :]) + shift[:, None, :]


def axial_rope(n_tokens):
    rows_count = n_tokens // GRID_W
    rows = jnp.repeat(jnp.arange(rows_count, dtype=jnp.float32), GRID_W)
    cols = jnp.tile(jnp.arange(GRID_W, dtype=jnp.float32), rows_count)
    inv = ROPE_THETA ** (-jnp.arange(ROPE_PAIRS_PER_AXIS, dtype=jnp.float32) / ROPE_PAIRS_PER_AXIS)
    ang = jnp.concatenate([rows[:, None] * inv, cols[:, None] * inv], axis=-1)
    return jnp.cos(ang), jnp.sin(ang)


def apply_rope(x, cos, sin):
    xf = x.astype(jnp.float32)
    x1, x2 = xf[..., : HEAD_DIM // 2], xf[..., HEAD_DIM // 2:]
    c = cos[None, :, None, :]
    s = sin[None, :, None, :]
    return jnp.concatenate([x1 * c - x2 * s, x2 * c + x1 * s], axis=-1).astype(x.dtype)


def q_heads(zq, q_norm):
    B, N = zq.shape[:2]
    return rms_norm(zq.reshape(B, N, N_HEADS, HEAD_DIM), q_norm)


def kv_heads(zk, zv, k_norm):
    B, N = zk.shape[:2]
    k = rms_norm(zk.reshape(B, N, N_KV_HEADS, HEAD_DIM), k_norm)
    v = zv.reshape(B, N, N_KV_HEADS, HEAD_DIM)
    return k, v


def latent_attention(q, k, v, k_ctx, v_ctx):
    B, S = q.shape[:2]
    L = k_ctx.shape[1]
    nb = S // Q_BLOCK
    qb = q.reshape(B, nb, Q_BLOCK, N_KV_HEADS, Q_PER_KV, HEAD_DIM).transpose(1, 0, 2, 3, 4, 5)

    def one_block(qblk):
        s_ctx = jnp.einsum('bqkgd,bskd->bkgqs', qblk, k_ctx).astype(jnp.float32)
        s_lat = jnp.einsum('bqkgd,bskd->bkgqs', qblk, k).astype(jnp.float32)
        s = jnp.concatenate([s_ctx, s_lat], axis=-1) * ATTN_SCALE
        p = jax.nn.softmax(s, axis=-1).astype(v.dtype)
        return (jnp.einsum('bkgqs,bskd->bqkgd', p[..., :L], v_ctx)
                + jnp.einsum('bkgqs,bskd->bqkgd', p[..., L:], v))

    o = lax.map(one_block, qb)
    return o.transpose(1, 0, 2, 3, 4, 5).reshape(B, S, ATTN_W)


def context_attention(q, k, v):
    B, L = q.shape[:2]
    qg = q.reshape(B, L, N_KV_HEADS, Q_PER_KV, HEAD_DIM)
    s = jnp.einsum('bqkgd,bskd->bkgqs', qg, k).astype(jnp.float32) * ATTN_SCALE
    p = jax.nn.softmax(s, axis=-1).astype(v.dtype)
    return jnp.einsum('bkgqs,bskd->bqkgd', p, v).reshape(B, L, ATTN_W)


def spatial_gating(zg, sgu_norm, w_s, b_s):
    B, N = zg.shape[:2]
    z = jax.nn.gelu(zg)
    u, v = z[..., :SGU_W], z[..., SGU_W:]
    v = rms_norm(v, sgu_norm).reshape(B, N // CHUNK, CHUNK, SGU_GROUPS, SGU_GROUP_W)
    mixed = jnp.einsum('hpq,bnqhc->bnphc', w_s, v) + b_s.T[:, :, None]
    return u * mixed.reshape(B, N, SGU_W)


def multiscale_pool(p, pool_w, pool_scale):
    B, N = p.shape[:2]
    pf = p.reshape(B, N, POOL_GROUPS, POOL_GROUP_W).astype(jnp.float32)
    cs = jnp.concatenate([jnp.zeros((B, 1, POOL_GROUPS, POOL_GROUP_W), jnp.float32),
                          jnp.cumsum(pf, axis=1)], axis=1)
    t = jnp.arange(N)
    means = []
    for g, w in enumerate(POOL_WINDOWS):
        lo = jnp.clip(t - w // 2, 0, N)
        hi = jnp.clip(t - w // 2 + w, 0, N)
        win_sum = cs[:, hi, g] - cs[:, lo, g]
        means.append(win_sum / (hi - lo).astype(jnp.float32)[None, :, None])
    d = (jnp.stack(means, axis=2) - pf).astype(p.dtype)
    out = jnp.einsum('bngc,gcd->bngd', d, pool_w).reshape(B, N, POOL_W)
    return out * pool_scale


def swiglu(h, w_gate, w_up, w_down):
    return (jax.nn.silu(h @ w_gate) * (h @ w_up)) @ w_down


def setup_inputs(seed: int = 0) -> dict:
    key = jax.random.key(seed)
    ks = jax.random.split(key, 24)
    f32 = jnp.float32

    def nrm(k, shape, s):
        return jax.random.normal(k, shape, f32) * s

    return {
        "x": nrm(ks[0], (BATCH, SEQ, D_MODEL), 1.0),
        "c": nrm(ks[1], (BATCH, D_MODEL), 1.0),
        "ctx": nrm(ks[2], (BATCH, CTX_LEN, D_MODEL), 1.0),
        "c_ctx": nrm(ks[3], (D_MODEL,), 1.0),
        "w_mod": nrm(ks[4], (DEPTH, D_MODEL, 6 * D_MODEL), 0.5 * D_MODEL ** -0.5),
        "b_mod": nrm(ks[5], (DEPTH, 6 * D_MODEL), 0.02),
        "norm1": 1.0 + nrm(ks[6], (DEPTH, D_MODEL), 0.02),
        "norm2": 1.0 + nrm(ks[7], (DEPTH, D_MODEL), 0.02),
        "w_in": nrm(ks[8], (DEPTH, D_MODEL, IN_W), D_MODEL ** -0.5),
        "q_norm": 1.0 + nrm(ks[9], (DEPTH, HEAD_DIM), 0.02),
        "k_norm": 1.0 + nrm(ks[10], (DEPTH, HEAD_DIM), 0.02),
        "sgu_norm": 1.0 + nrm(ks[11], (DEPTH, SGU_W), 0.02),
        "w_s": nrm(ks[12], (DEPTH, SGU_GROUPS, CHUNK, CHUNK), CHUNK ** -0.5),
        "b_s": 1.0 + nrm(ks[13], (DEPTH, SGU_GROUPS, CHUNK), 0.02),
        "pool_w": nrm(ks[14], (DEPTH, POOL_GROUPS, POOL_GROUP_W, POOL_GROUP_W), POOL_GROUP_W ** -0.5),
        "pool_scale": 1.0 + nrm(ks[15], (DEPTH, POOL_W), 0.02),
        "w_out": nrm(ks[16], (DEPTH, MIX_W, D_MODEL), MIX_W ** -0.5),
        "w_gate": nrm(ks[17], (DEPTH, D_MODEL, D_FF), D_MODEL ** -0.5),
        "w_up": nrm(ks[18], (DEPTH, D_MODEL, D_FF), D_MODEL ** -0.5),
        "w_down": nrm(ks[19], (DEPTH, D_FF, D_MODEL), D_FF ** -0.5),
        "final_norm": 1.0 + nrm(ks[20], (D_MODEL,), 0.02),
    }


def reference(x, c, ctx, c_ctx, w_mod, b_mod, norm1, norm2, w_in, q_norm, k_norm,
              sgu_norm, w_s, b_s, pool_w, pool_scale, w_out, w_gate, w_up, w_down,
              final_norm):
    cos, sin = axial_rope(x.shape[1])
    for l in range(DEPTH):
        last = l == DEPTH - 1
        mod = jax.nn.silu(c) @ w_mod[l] + b_mod[l]
        mod_c = jax.nn.silu(c_ctx)[None, :] @ w_mod[l] + b_mod[l]
        sh1, sc1, gt1, sh2, sc2, gt2 = jnp.split(mod, 6, axis=-1)
        csh1, csc1, cgt1, csh2, csc2, cgt2 = jnp.split(mod_c, 6, axis=-1)

        h = modulate(x, norm1[l], sh1, sc1)
        hc = modulate(ctx, norm1[l], csh1, csc1)
        z = h @ w_in[l]
        q = apply_rope(q_heads(z[..., O_Q:O_K], q_norm[l]), cos, sin)
        k, v = kv_heads(z[..., O_K:O_V], z[..., O_V:O_G], k_norm[l])
        k = apply_rope(k, cos, sin)

        if last:
            zc_kv = hc @ w_in[l][:, O_K:O_G]
            k_c, v_c = kv_heads(zc_kv[..., :KV_W], zc_kv[..., KV_W:], k_norm[l])
        else:
            zc = hc @ w_in[l]
            k_c, v_c = kv_heads(zc[..., O_K:O_V], zc[..., O_V:O_G], k_norm[l])

        attn = latent_attention(q, k, v, k_c, v_c)
        sgu = spatial_gating(z[..., O_G:O_P], sgu_norm[l], w_s[l], b_s[l])
        pool = multiscale_pool(z[..., O_P:], pool_w[l], pool_scale[l])
        mix = jnp.concatenate([attn, sgu, pool], axis=-1) @ w_out[l]

        if not last:
            q_c = q_heads(zc[..., O_Q:O_K], q_norm[l])
            attn_c = context_attention(q_c, k_c, v_c)
            sgu_c = spatial_gating(zc[..., O_G:O_P], sgu_norm[l], w_s[l], b_s[l])
            pool_c = multiscale_pool(zc[..., O_P:], pool_w[l], pool_scale[l])
            mix_c = jnp.concatenate([attn_c, sgu_c, pool_c], axis=-1) @ w_out[l]
            ctx = ctx + cgt1[:, None, :] * mix_c
            hc2 = modulate(ctx, norm2[l], csh2, csc2)
            ctx = ctx + cgt2[:, None, :] * swiglu(hc2, w_gate[l], w_up[l], w_down[l])

        x = x + gt1[:, None, :] * mix
        h2 = modulate(x, norm2[l], sh2, sc2)
        x = x + gt2[:, None, :] * swiglu(h2, w_gate[l], w_up[l], w_down[l])

    return rms_norm(x, final_norm)
```

```python
import functools
import math

import jax
import jax.numpy as jnp
from jax import lax
from jax.experimental import pallas as pl
from jax.experimental.pallas import tpu as pltpu

F32 = jnp.float32
BF16 = jnp.bfloat16

D_MODEL = 1024
DEPTH = 2
GRID_W = 64
EPS = 1e-6

HEAD_DIM = 64
N_HEADS = 8
N_KV_HEADS = 2
Q_PER_KV = N_HEADS // N_KV_HEADS
ATTN_W = N_HEADS * HEAD_DIM
KV_W = N_KV_HEADS * HEAD_DIM
ROPE_THETA = 10000.0
ROPE_PAIRS_PER_AXIS = HEAD_DIM // 4
ATTN_SCALE = HEAD_DIM ** -0.5

CHUNK = 128
SGU_W = D_MODEL // 4
SGU_GROUPS = 4
POOL_WINDOWS = (2, 4, 8, 16)
POOL_W = D_MODEL // 4
POOL_HALO = 8

MIX_W = ATTN_W + SGU_W + POOL_W
O_Q = 0
O_K = O_Q + ATTN_W
O_V = O_K + KV_W
O_G = O_V + KV_W
O_P = O_G + 2 * SGU_W
IN_W = O_P + POOL_W
D_FF = 2816

LANES = 128
MOD_ROWS = 16
VMEM_LIMIT = 56 * 1024 * 1024

TOKEN_TILE = 256
Q_TILE = 256


def _half_mask(shape):
    lane = lax.broadcasted_iota(jnp.int32, shape, len(shape) - 1)
    return (lane & HEAD_DIM) == 0


def _rms(x, gain):
    return x * lax.rsqrt(jnp.mean(x * x, axis=-1, keepdims=True) + EPS) * gain


def _head_rms_slab(z, gain):
    lo = _half_mask(z.shape)
    sq = z * z
    s0 = jnp.sum(jnp.where(lo, sq, 0.0), axis=-1, keepdims=True)
    s1 = jnp.sum(jnp.where(lo, 0.0, sq), axis=-1, keepdims=True)
    ms = jnp.where(lo, s0, s1) * (1.0 / HEAD_DIM)
    return z * lax.rsqrt(ms + EPS) * gain


def _rope_slab(x, cos, sin_signed):
    lane = lax.broadcasted_iota(jnp.int32, x.shape, 1)
    first = (lane & (HEAD_DIM // 2)) == 0
    up = pltpu.roll(x, LANES - HEAD_DIM // 2, axis=1)
    down = pltpu.roll(x, HEAD_DIM // 2, axis=1)
    partner = jnp.where(first, up, down)
    return x * cos + partner * sin_signed


def _gelu_tanh(x):
    c = math.sqrt(2.0 / math.pi)
    return 0.5 * x * (1.0 + jnp.tanh(c * (x + 0.044715 * (x * x * x))))


def _silu(x):
    return x / (1.0 + jnp.exp(-x))


def _mod_kernel(c_ref, w_ref, b_ref, o_ref):
    a = _silu(c_ref[...])
    o_ref[0] = jnp.dot(a, w_ref[0], preferred_element_type=F32,
                       precision=lax.Precision.HIGHEST) + b_ref[0]


def _modulation(c_rows, w_mod, b_mod):
    n_col = 6
    return pl.pallas_call(
        _mod_kernel,
        grid=(DEPTH, n_col),
        in_specs=[
            pl.BlockSpec((MOD_ROWS, D_MODEL), lambda l, j: (0, 0)),
            pl.BlockSpec((1, D_MODEL, D_MODEL), lambda l, j: (l, 0, j)),
            pl.BlockSpec((1, 1, D_MODEL), lambda l, j: (l, 0, j)),
        ],
        out_specs=pl.BlockSpec((1, MOD_ROWS, D_MODEL), lambda l, j: (l, 0, j)),
        out_shape=jax.ShapeDtypeStruct((DEPTH, MOD_ROWS, 6 * D_MODEL), F32),
        compiler_params=pltpu.CompilerParams(dimension_semantics=("parallel", "parallel")),
        name="modulation",
    )(c_rows, w_mod, b_mod.reshape(DEPTH, 1, 6 * D_MODEL))


def _inproj_kernel(x_ref, mod_ref, n1_ref, w_ref, qg_ref, kg_ref, sg_ref, cos_ref, sin_ref,
                   q_ref, kt_ref, vd_ref, u_ref, vn_ref, p_ref, *, rope):
    x = x_ref[0]
    m = mod_ref[0]
    h = _rms(x, n1_ref[...]) * (1.0 + m[1:2]) + m[0:1]
    z = jnp.dot(h.astype(BF16), w_ref[...], preferred_element_type=F32)

    if rope:
        cos = cos_ref[...]
        sin = sin_ref[...]

    for j in range(ATTN_W // LANES):
        zq = z[:, O_Q + j * LANES:O_Q + (j + 1) * LANES]
        qn = _head_rms_slab(zq, qg_ref[...])
        if rope:
            qn = _rope_slab(qn, cos, sin)
        q_ref[0, :, j * LANES:(j + 1) * LANES] = (qn * ATTN_SCALE).astype(BF16)

    kn = _head_rms_slab(z[:, O_K:O_K + KV_W], kg_ref[...])
    if rope:
        kn = _rope_slab(kn, cos, sin)
    v = z[:, O_V:O_V + KV_W]
    lo = _half_mask(kn.shape)
    k_sw = pltpu.roll(kn, HEAD_DIM, axis=1)
    v_sw = pltpu.roll(v, HEAD_DIM, axis=1)
    kt_ref[0, 0] = jnp.where(lo, kn, k_sw).T.astype(BF16)
    kt_ref[0, 1] = jnp.where(lo, k_sw, kn).T.astype(BF16)
    vd_ref[0, 0] = jnp.where(lo, v, v_sw).astype(BF16)
    vd_ref[0, 1] = jnp.where(lo, v_sw, v).astype(BF16)

    g = _gelu_tanh(z[:, O_G:O_P])
    u_ref[0] = g[:, :SGU_W]
    vn_ref[0] = _rms(g[:, SGU_W:], sg_ref[...]).astype(BF16)

    p_ref[0] = z[:, O_P:]


def _inproj(x, mod_l, mod_row_of_batch, norm1, w_in, qg, kg, sg, cos, sin, *, rope):
    B, N, _ = x.shape
    tm = min(TOKEN_TILE, N)
    grid = (B, N // tm)
    const = lambda b, i: (0, 0)
    return pl.pallas_call(
        functools.partial(_inproj_kernel, rope=rope),
        grid=grid,
        in_specs=[
            pl.BlockSpec((1, tm, D_MODEL), lambda b, i: (b, i, 0)),
            pl.BlockSpec((1, 6, D_MODEL), lambda b, i: (mod_row_of_batch(b), 0, 0)),
            pl.BlockSpec((1, D_MODEL), const),
            pl.BlockSpec((D_MODEL, IN_W), const),
            pl.BlockSpec((1, LANES), const),
            pl.BlockSpec((1, LANES), const),
            pl.BlockSpec((1, SGU_W), const),
            pl.BlockSpec((tm, LANES), lambda b, i: (i, 0)),
            pl.BlockSpec((tm, LANES), lambda b, i: (i, 0)),
        ],
        out_specs=[
            pl.BlockSpec((1, tm, ATTN_W), lambda b, i: (b, i, 0)),
            pl.BlockSpec((1, N_KV_HEADS, LANES, tm), lambda b, i: (b, 0, 0, i)),
            pl.BlockSpec((1, N_KV_HEADS, tm, LANES), lambda b, i: (b, 0, i, 0)),
            pl.BlockSpec((1, tm, SGU_W), lambda b, i: (b, i, 0)),
            pl.BlockSpec((1, tm, SGU_W), lambda b, i: (b, i, 0)),
            pl.BlockSpec((1, tm, POOL_W), lambda b, i: (b, i, 0)),
        ],
        out_shape=[
            jax.ShapeDtypeStruct((B, N, ATTN_W), BF16),
            jax.ShapeDtypeStruct((B, N_KV_HEADS, LANES, N), BF16),
            jax.ShapeDtypeStruct((B, N_KV_HEADS, N, LANES), BF16),
            jax.ShapeDtypeStruct((B, N, SGU_W), F32),
            jax.ShapeDtypeStruct((B, N, SGU_W), BF16),
            jax.ShapeDtypeStruct((B, N, POOL_W), F32),
        ],
        compiler_params=pltpu.CompilerParams(
            dimension_semantics=("parallel", "parallel"), vmem_limit_bytes=VMEM_LIMIT),
        name="inproj_rope" if rope else "inproj",
    )(x, mod_l, norm1, w_in, qg, kg, sg, cos, sin)


def _attn_kernel(*refs, n_kv_sets):
    q_ref = refs[0]
    kt_refs = refs[1:1 + n_kv_sets]
    vd_refs = refs[1 + n_kv_sets:1 + 2 * n_kv_sets]
    o_ref = refs[1 + 2 * n_kv_sets]
    q = q_ref[0]
    lo = _half_mask((q.shape[0], LANES))
    zero = jnp.zeros((q.shape[0], LANES), BF16)
    for j in range(Q_PER_KV * HEAD_DIM // LANES):
        qs = q[:, j * LANES:(j + 1) * LANES]
        halves = []
        for half in range(2):
            qm = jnp.where(lo, qs, zero) if half == 0 else jnp.where(lo, zero, qs)
            s = [jnp.dot(qm, kt[0, 0], preferred_element_type=F32) for kt in kt_refs]
            mx = functools.reduce(jnp.maximum, [jnp.max(si, axis=-1, keepdims=True) for si in s])
            p = [jnp.exp(si - mx) for si in s]
            den = functools.reduce(lambda a, b: a + b, [jnp.sum(pi, axis=-1, keepdims=True) for pi in p])
            o = functools.reduce(lambda a, b: a + b, [
                jnp.dot(pi.astype(BF16), vd[0, 0], preferred_element_type=F32)
                for pi, vd in zip(p, vd_refs)])
            halves.append(o / den)
        o_ref[0, :, j * LANES:(j + 1) * LANES] = jnp.where(lo, halves[0], halves[1]).astype(BF16)


def _attention(q, kv_sets):
    B, N, _ = q.shape
    tq = min(Q_TILE, N)
    gw = Q_PER_KV * HEAD_DIM
    in_specs = [pl.BlockSpec((1, tq, gw), lambda b, h, i: (b, i, h))]
    for kt, _ in kv_sets:
        in_specs.append(pl.BlockSpec((1, 1, LANES, kt.shape[3]), lambda b, h, i: (b, h, 0, 0)))
    for _, vd in kv_sets:
        in_specs.append(pl.BlockSpec((1, 1, vd.shape[2], LANES), lambda b, h, i: (b, h, 0, 0)))
    return pl.pallas_call(
        functools.partial(_attn_kernel, n_kv_sets=len(kv_sets)),
        grid=(B, N_KV_HEADS, N // tq),
        in_specs=in_specs,
        out_specs=pl.BlockSpec((1, tq, gw), lambda b, h, i: (b, i, h)),
        out_shape=jax.ShapeDtypeStruct((B, N, ATTN_W), BF16),
        compiler_params=pltpu.CompilerParams(
            dimension_semantics=("parallel", "parallel", "parallel"), vmem_limit_bytes=VMEM_LIMIT),
        name=f"attention_{len(kv_sets)}",
    )(q, *[kt for kt, _ in kv_sets], *[vd for _, vd in kv_sets])


def _mixffn_kernel(x_ref, mod_ref, attn_ref, u_ref, vn_ref, p_ref, pprev_ref, pnext_ref,
                   ws_ref, bs_ref, pw_ref, ps_ref, wo_ref, n2_ref, wg_ref, wu_ref, wd_ref,
                   fn_ref, o_ref, *, seq_len, final):
    i = pl.program_id(1)
    n_tiles = pl.num_programs(1)
    tm = x_ref.shape[1]
    m = mod_ref[0]

    lo = _half_mask((CHUNK, LANES))
    sgu_chunks = []
    for c in range(tm // CHUNK):
        slabs = []
        for j in range(SGU_W // LANES):
            vs = vn_ref[0, c * CHUNK:(c + 1) * CHUNK, j * LANES:(j + 1) * LANES]
            ra = jnp.dot(ws_ref[2 * j], vs, preferred_element_type=F32)
            rb = jnp.dot(ws_ref[2 * j + 1], vs, preferred_element_type=F32)
            slabs.append(jnp.where(lo, ra, rb))
        mixed = jnp.concatenate(slabs, axis=-1) + bs_ref[...]
        sgu_chunks.append(u_ref[0, c * CHUNK:(c + 1) * CHUNK, :] * mixed)
    sgu = jnp.concatenate(sgu_chunks, axis=0)

    p = p_ref[0]
    prev = jnp.where(i > 0, pprev_ref[0], 0.0)
    nxt = jnp.where(i < n_tiles - 1, pnext_ref[0], 0.0)
    e = jnp.concatenate([prev, p, nxt], axis=0)
    n_ext = tm + 2 * POOL_HALO

    def ahead(a, k):
        return pltpu.roll(a, n_ext - k, axis=0)

    a2 = e + ahead(e, 1)
    a4 = a2 + ahead(a2, 2)
    a8 = a4 + ahead(a4, 4)
    a16 = a8 + ahead(a8, 8)
    s2 = ahead(a2, POOL_HALO - 1)[:tm]
    s4 = ahead(a4, POOL_HALO - 2)[:tm]
    s8 = ahead(a8, POOL_HALO - 4)[:tm]
    s16 = a16[:tm]
    grp = lax.broadcasted_iota(jnp.int32, (tm, POOL_W), 1) >> 6
    win_sum = jnp.where(grp == 0, s2, jnp.where(grp == 1, s4, jnp.where(grp == 2, s8, s16)))
    t = i * tm + lax.broadcasted_iota(jnp.int32, (tm, POOL_W), 0)
    half = jnp.left_shift(1, grp)
    cnt = jnp.minimum(t + half, seq_len) - jnp.maximum(t - half, 0)
    d = win_sum / cnt.astype(F32) - p
    pool = jnp.dot(d.astype(BF16), pw_ref[...], preferred_element_type=F32) * ps_ref[...]

    cat = jnp.concatenate([attn_ref[0], sgu.astype(BF16), pool.astype(BF16)], axis=-1)
    mix = jnp.dot(cat, wo_ref[...], preferred_element_type=F32)
    x1 = x_ref[0] + m[2:3] * mix

    h2 = (_rms(x1, n2_ref[...]) * (1.0 + m[4:5]) + m[3:4]).astype(BF16)
    gate = jnp.dot(h2, wg_ref[...], preferred_element_type=F32)
    up = jnp.dot(h2, wu_ref[...], preferred_element_type=F32)
    act = (_silu(gate) * up).astype(BF16)
    x2 = x1 + m[5:6] * jnp.dot(act, wd_ref[...], preferred_element_type=F32)
    if final:
        x2 = _rms(x2, fn_ref[...])
    o_ref[0] = x2


def _mixffn(x, mod_l, mod_row_of_batch, attn, u, vn, p, ws, bs, pw, ps, wo, n2, wg, wu, wd, fn,
            *, final):
    B, N, _ = x.shape
    tm = min(TOKEN_TILE, N)
    hb = tm // POOL_HALO
    n_hblocks = N // POOL_HALO
    const2 = lambda b, i: (0, 0)
    const3 = lambda b, i: (0, 0, 0)
    once = pl.Buffered(1)
    tile = lambda w: pl.BlockSpec((1, tm, w), lambda b, i: (b, i, 0))
    return pl.pallas_call(
        functools.partial(_mixffn_kernel, seq_len=N, final=final),
        grid=(B, N // tm),
        in_specs=[
            tile(D_MODEL),
            pl.BlockSpec((1, 6, D_MODEL), lambda b, i: (mod_row_of_batch(b), 0, 0)),
            tile(ATTN_W), tile(SGU_W), tile(SGU_W), tile(POOL_W),
            pl.BlockSpec((1, POOL_HALO, POOL_W), lambda b, i: (b, jnp.maximum(i * hb - 1, 0), 0)),
            pl.BlockSpec((1, POOL_HALO, POOL_W),
                         lambda b, i: (b, jnp.minimum((i + 1) * hb, n_hblocks - 1), 0)),
            pl.BlockSpec((SGU_GROUPS, CHUNK, CHUNK), const3, pipeline_mode=once),
            pl.BlockSpec((CHUNK, SGU_W), const2, pipeline_mode=once),
            pl.BlockSpec((POOL_W, POOL_W), const2, pipeline_mode=once),
            pl.BlockSpec((1, POOL_W), const2, pipeline_mode=once),
            pl.BlockSpec((MIX_W, D_MODEL), const2, pipeline_mode=once),
            pl.BlockSpec((1, D_MODEL), const2, pipeline_mode=once),
            pl.BlockSpec((D_MODEL, D_FF), const2, pipeline_mode=once),
            pl.BlockSpec((D_MODEL, D_FF), const2, pipeline_mode=once),
            pl.BlockSpec((D_FF, D_MODEL), const2, pipeline_mode=once),
            pl.BlockSpec((1, D_MODEL), const2, pipeline_mode=once),
        ],
        out_specs=tile(D_MODEL),
        out_shape=jax.ShapeDtypeStruct((B, N, D_MODEL), F32),
        compiler_params=pltpu.CompilerParams(
            dimension_semantics=("parallel", "parallel"), vmem_limit_bytes=VMEM_LIMIT),
        name="mixffn_final" if final else "mixffn",
    )(x, mod_l, attn, u, vn, p, p, p, ws, bs, pw, ps, wo, n2, wg, wu, wd, fn)


def _rope_tables(n_tokens):
    rows = jnp.repeat(jnp.arange(n_tokens // GRID_W, dtype=F32), GRID_W)
    cols = jnp.tile(jnp.arange(GRID_W, dtype=F32), n_tokens // GRID_W)
    inv = ROPE_THETA ** (-jnp.arange(ROPE_PAIRS_PER_AXIS, dtype=F32) / ROPE_PAIRS_PER_AXIS)
    ang = jnp.concatenate([rows[:, None] * inv, cols[:, None] * inv], axis=-1)
    cos, sin = jnp.cos(ang), jnp.sin(ang)
    return jnp.tile(cos, (1, 4)), jnp.tile(jnp.concatenate([-sin, sin], axis=-1), (1, 2))


def _block_diag(w):
    g, a, b = w.shape
    out = jnp.zeros((g * a, g * b), w.dtype)
    for k in range(g):
        out = out.at[k * a:(k + 1) * a, k * b:(k + 1) * b].set(w[k])
    return out


def kernel(x, c, ctx, c_ctx, w_mod, b_mod, norm1, norm2, w_in, q_norm, k_norm, sgu_norm, w_s, b_s,
           pool_w, pool_scale, w_out, w_gate, w_up, w_down, final_norm):
    B, S, _ = x.shape
    L = ctx.shape[1]
    c_rows = jnp.concatenate(
        [c, c_ctx[None, :], jnp.zeros((MOD_ROWS - B - 1, D_MODEL), F32)], axis=0)
    mod = _modulation(c_rows, w_mod, b_mod).reshape(DEPTH, MOD_ROWS, 6, D_MODEL)
    cos, sin = _rope_tables(S)
    ones = jnp.ones((L, LANES), F32)
    lat_row = lambda b: b
    ctx_row = lambda b: B

    for l in range(DEPTH):
        last = l == DEPTH - 1
        w_in_l = w_in[l].astype(BF16)
        qg = jnp.tile(q_norm[l], 2)[None, :]
        kg = jnp.tile(k_norm[l], 2)[None, :]
        sg = sgu_norm[l][None, :]
        n1 = norm1[l][None, :]
        mix_w = (
            w_s[l].astype(BF16),
            jnp.repeat(b_s[l].T, SGU_W // SGU_GROUPS, axis=1),
            _block_diag(pool_w[l]).astype(BF16),
            pool_scale[l][None, :],
            w_out[l].astype(BF16),
            norm2[l][None, :],
            w_gate[l].astype(BF16),
            w_up[l].astype(BF16),
            w_down[l].astype(BF16),
            final_norm[None, :],
        )

        q, kt, vd, u, vn, p = _inproj(x, mod[l], lat_row, n1, w_in_l, qg, kg, sg, cos, sin,
                                      rope=True)
        qc, ktc, vdc, uc, vnc, pc = _inproj(ctx, mod[l], ctx_row, n1, w_in_l, qg, kg, sg,
                                            ones, ones, rope=False)
        attn = _attention(q, [(ktc, vdc), (kt, vd)])
        if not last:
            attn_c = _attention(qc, [(ktc, vdc)])
            ctx = _mixffn(ctx, mod[l], ctx_row, attn_c, uc, vnc, pc, *mix_w, final=False)
        x = _mixffn(x, mod[l], lat_row, attn, u, vn, p, *mix_w, final=last)
    return x
```

```python
import functools
import math

import jax
import jax.numpy as jnp
from jax import lax
from jax.experimental import pallas as pl
from jax.experimental.pallas import tpu as pltpu

F32 = jnp.float32
BF16 = jnp.bfloat16

D_MODEL = 1024
DEPTH = 2
GRID_W = 64
EPS = 1e-6

HEAD_DIM = 64
N_HEADS = 8
N_KV_HEADS = 2
Q_PER_KV = N_HEADS // N_KV_HEADS
ATTN_W = N_HEADS * HEAD_DIM
KV_W = N_KV_HEADS * HEAD_DIM
ROPE_THETA = 10000.0
ROPE_PAIRS_PER_AXIS = HEAD_DIM // 4
ATTN_SCALE = HEAD_DIM ** -0.5
LOG2E = math.log2(math.e)

CHUNK = 128
SGU_W = D_MODEL // 4
SGU_GROUPS = 4
POOL_WINDOWS = (2, 4, 8, 16)
POOL_W = D_MODEL // 4
POOL_HALO = 8

MIX_W = ATTN_W + SGU_W + POOL_W
O_Q = 0
O_K = O_Q + ATTN_W
O_V = O_K + KV_W
O_G = O_V + KV_W
O_P = O_G + 2 * SGU_W
IN_W = O_P + POOL_W
D_FF = 2816

LANES = 128
MOD_ROWS = 16
VMEM_LIMIT = 56 * 1024 * 1024

TOKEN_TILE = 256
Q_TILE = 512


def _half_mask(shape):
    lane = lax.broadcasted_iota(jnp.int32, shape, len(shape) - 1)
    return (lane & HEAD_DIM) == 0


def _rms(x, gain):
    return x * lax.rsqrt(jnp.mean(x * x, axis=-1, keepdims=True) + EPS) * gain


def _head_rms_slab(z, gain):
    lo = _half_mask(z.shape)
    sq = z * z
    s0 = jnp.sum(jnp.where(lo, sq, 0.0), axis=-1, keepdims=True)
    s1 = jnp.sum(jnp.where(lo, 0.0, sq), axis=-1, keepdims=True)
    ms = jnp.where(lo, s0, s1) * (1.0 / HEAD_DIM)
    return z * lax.rsqrt(ms + EPS) * gain


def _rope_slab(x, cos, sin_signed):
    lane = lax.broadcasted_iota(jnp.int32, x.shape, 1)
    first = (lane & (HEAD_DIM // 2)) == 0
    up = pltpu.roll(x, LANES - HEAD_DIM // 2, axis=1)
    down = pltpu.roll(x, HEAD_DIM // 2, axis=1)
    partner = jnp.where(first, up, down)
    return x * cos + partner * sin_signed


def _gelu_tanh(x):
    c = math.sqrt(2.0 / math.pi)
    return 0.5 * x * (1.0 + jnp.tanh(c * (x + 0.044715 * (x * x * x))))


def _silu(x):
    return x / (1.0 + jnp.exp(-x))


def _mod_kernel(c_ref, w_ref, b_ref, o_ref):
    a = _silu(c_ref[...])
    o_ref[0] = jnp.dot(a, w_ref[0], preferred_element_type=F32,
                       precision=lax.Precision.HIGHEST) + b_ref[0]


def _modulation(c_rows, w_mod, b_mod):
    n_col = 6
    return pl.pallas_call(
        _mod_kernel,
        grid=(DEPTH, n_col),
        in_specs=[
            pl.BlockSpec((MOD_ROWS, D_MODEL), lambda l, j: (0, 0)),
            pl.BlockSpec((1, D_MODEL, D_MODEL), lambda l, j: (l, 0, j)),
            pl.BlockSpec((1, 1, D_MODEL), lambda l, j: (l, 0, j)),
        ],
        out_specs=pl.BlockSpec((1, MOD_ROWS, D_MODEL), lambda l, j: (l, 0, j)),
        out_shape=jax.ShapeDtypeStruct((DEPTH, MOD_ROWS, 6 * D_MODEL), F32),
        compiler_params=pltpu.CompilerParams(dimension_semantics=("parallel", "parallel")),
        name="modulation",
    )(c_rows, w_mod, b_mod.reshape(DEPTH, 1, 6 * D_MODEL))


def _inproj_kernel(x_ref, mod_ref, n1_ref, w_ref, qg_ref, kg_ref, sg_ref, cos_ref, sin_ref,
                   q_ref, k_ref, vt_ref, u_ref, vn_ref, p_ref, *, rope):
    x = x_ref[0]
    m = mod_ref[0]
    h = _rms(x, n1_ref[...]) * (1.0 + m[1:2]) + m[0:1]
    z = jnp.dot(h.astype(BF16), w_ref[...], preferred_element_type=F32)

    if rope:
        cos = cos_ref[...]
        sin = sin_ref[...]

    for j in range(ATTN_W // LANES):
        zq = z[:, O_Q + j * LANES:O_Q + (j + 1) * LANES]
        qn = _head_rms_slab(zq, qg_ref[...])
        if rope:
            qn = _rope_slab(qn, cos, sin)
        q_ref[0, :, j * LANES:(j + 1) * LANES] = (qn * (ATTN_SCALE * LOG2E)).astype(BF16)

    kn = _head_rms_slab(z[:, O_K:O_K + KV_W], kg_ref[...])
    if rope:
        kn = _rope_slab(kn, cos, sin)
    k_ref[0] = kn.astype(BF16)

    vt = z[:, O_V:O_V + KV_W].T
    ones = jnp.ones((HEAD_DIM, vt.shape[1]), F32)
    for hh in range(N_KV_HEADS):
        vt_ref[0, hh] = jnp.concatenate(
            [vt[hh * HEAD_DIM:(hh + 1) * HEAD_DIM], ones], axis=0).astype(BF16)

    g = _gelu_tanh(z[:, O_G:O_P])
    u_ref[0] = g[:, :SGU_W]
    vn_ref[0] = _rms(g[:, SGU_W:], sg_ref[...]).astype(BF16)

    p_ref[0] = z[:, O_P:]


def _inproj(x, mod_l, mod_row_of_batch, norm1, w_in, qg, kg, sg, cos, sin, *, rope):
    B, N, _ = x.shape
    tm = min(TOKEN_TILE, N)
    grid = (B, N // tm)
    const = lambda b, i: (0, 0)
    return pl.pallas_call(
        functools.partial(_inproj_kernel, rope=rope),
        grid=grid,
        in_specs=[
            pl.BlockSpec((1, tm, D_MODEL), lambda b, i: (b, i, 0)),
            pl.BlockSpec((1, 6, D_MODEL), lambda b, i: (mod_row_of_batch(b), 0, 0)),
            pl.BlockSpec((1, D_MODEL), const),
            pl.BlockSpec((D_MODEL, IN_W), const),
            pl.BlockSpec((1, LANES), const),
            pl.BlockSpec((1, LANES), const),
            pl.BlockSpec((1, SGU_W), const),
            pl.BlockSpec((tm, LANES), lambda b, i: (i, 0)),
            pl.BlockSpec((tm, LANES), lambda b, i: (i, 0)),
        ],
        out_specs=[
            pl.BlockSpec((1, tm, ATTN_W), lambda b, i: (b, i, 0)),
            pl.BlockSpec((1, tm, KV_W), lambda b, i: (b, i, 0)),
            pl.BlockSpec((1, N_KV_HEADS, 2 * HEAD_DIM, tm), lambda b, i: (b, 0, 0, i)),
            pl.BlockSpec((1, tm, SGU_W), lambda b, i: (b, i, 0)),
            pl.BlockSpec((1, tm, SGU_W), lambda b, i: (b, i, 0)),
            pl.BlockSpec((1, tm, POOL_W), lambda b, i: (b, i, 0)),
        ],
        out_shape=[
            jax.ShapeDtypeStruct((B, N, ATTN_W), BF16),
            jax.ShapeDtypeStruct((B, N, KV_W), BF16),
            jax.ShapeDtypeStruct((B, N_KV_HEADS, 2 * HEAD_DIM, N), BF16),
            jax.ShapeDtypeStruct((B, N, SGU_W), F32),
            jax.ShapeDtypeStruct((B, N, SGU_W), BF16),
            jax.ShapeDtypeStruct((B, N, POOL_W), F32),
        ],
        compiler_params=pltpu.CompilerParams(
            dimension_semantics=("parallel", "parallel"), vmem_limit_bytes=VMEM_LIMIT),
        name="inproj_rope" if rope else "inproj",
    )(x, mod_l, norm1, w_in, qg, kg, sg, cos, sin)


def _attn_kernel(*refs, n_kv_sets):
    q_ref = refs[0]
    k_refs = refs[1:1 + n_kv_sets]
    vt_refs = refs[1 + n_kv_sets:1 + 2 * n_kv_sets]
    o_ref = refs[1 + 2 * n_kv_sets]
    tq = q_ref.shape[1]
    lo = _half_mask((tq, LANES))
    zero = jnp.zeros((tq, LANES), BF16)
    nt = (((1,), (1,)), ((), ()))
    def scores(idx):
        s, hh = idx // N_KV_HEADS, idx % N_KV_HEADS
        qs = q_ref[0, :, s * LANES:(s + 1) * LANES]
        qm = jnp.where(lo, qs, zero) if hh == 0 else jnp.where(lo, zero, qs)
        st = [lax.dot_general(k[0], qm, nt, preferred_element_type=F32) for k in k_refs]
        mx = functools.reduce(jnp.maximum, [jnp.max(si, axis=0, keepdims=True) for si in st])
        return st, mx

    def weighted_values(idx, st, mx):
        hh = idx % N_KV_HEADS
        acc = functools.reduce(lambda a, b: a + b, [
            jnp.dot(vt[0, hh], jnp.exp2(si - mx).astype(BF16), preferred_element_type=F32)
            for si, vt in zip(st, vt_refs)])
        return acc[:HEAD_DIM] / acc[HEAD_DIM:HEAD_DIM + 1]

    heads = []
    pending = scores(0)
    for idx in range(N_HEADS):
        nxt = scores(idx + 1) if idx + 1 < N_HEADS else None
        heads.append(weighted_values(idx, *pending))
        pending = nxt
    o_ref[0] = jnp.concatenate(heads, axis=0).T.astype(BF16)


def _attention(q, kv_sets):
    B, N, _ = q.shape
    tq = min(Q_TILE, N)
    in_specs = [pl.BlockSpec((1, tq, ATTN_W), lambda b, i: (b, i, 0))]
    for k, _ in kv_sets:
        in_specs.append(pl.BlockSpec((1, k.shape[1], KV_W), lambda b, i: (b, 0, 0)))
    for _, vt in kv_sets:
        in_specs.append(pl.BlockSpec((1, N_KV_HEADS, 2 * HEAD_DIM, vt.shape[3]),
                                     lambda b, i: (b, 0, 0, 0)))
    return pl.pallas_call(
        functools.partial(_attn_kernel, n_kv_sets=len(kv_sets)),
        grid=(B, N // tq),
        in_specs=in_specs,
        out_specs=pl.BlockSpec((1, tq, ATTN_W), lambda b, i: (b, i, 0)),
        out_shape=jax.ShapeDtypeStruct((B, N, ATTN_W), BF16),
        compiler_params=pltpu.CompilerParams(
            dimension_semantics=("parallel", "parallel"), vmem_limit_bytes=VMEM_LIMIT),
        name=f"attention_{len(kv_sets)}",
    )(q, *[k for k, _ in kv_sets], *[vt for _, vt in kv_sets])


def _mixffn_kernel(x_ref, mod_ref, attn_ref, u_ref, vn_ref, p_ref, pprev_ref, pnext_ref,
                   ws_ref, bs_ref, pw_ref, ps_ref, wo_ref, n2_ref, wg_ref, wu_ref, wd_ref,
                   fn_ref, o_ref, *, seq_len, final):
    i = pl.program_id(1)
    n_tiles = pl.num_programs(1)
    tm = x_ref.shape[1]
    m = mod_ref[0]

    lo = _half_mask((CHUNK, LANES))
    sgu_chunks = []
    for c in range(tm // CHUNK):
        slabs = []
        for j in range(SGU_W // LANES):
            vs = vn_ref[0, c * CHUNK:(c + 1) * CHUNK, j * LANES:(j + 1) * LANES]
            ra = jnp.dot(ws_ref[2 * j], vs, preferred_element_type=F32)
            rb = jnp.dot(ws_ref[2 * j + 1], vs, preferred_element_type=F32)
            slabs.append(jnp.where(lo, ra, rb))
        mixed = jnp.concatenate(slabs, axis=-1) + bs_ref[...]
        sgu_chunks.append(u_ref[0, c * CHUNK:(c + 1) * CHUNK, :] * mixed)
    sgu = jnp.concatenate(sgu_chunks, axis=0)

    p = p_ref[0]
    prev = jnp.where(i > 0, pprev_ref[0], 0.0)
    nxt = jnp.where(i < n_tiles - 1, pnext_ref[0], 0.0)
    e = jnp.concatenate([prev, p, nxt], axis=0)
    n_ext = tm + 2 * POOL_HALO

    def ahead(a, k):
        return pltpu.roll(a, n_ext - k, axis=0)

    a2 = e + ahead(e, 1)
    a4 = a2 + ahead(a2, 2)
    a8 = a4 + ahead(a4, 4)
    a16 = a8 + ahead(a8, 8)
    s2 = ahead(a2, POOL_HALO - 1)[:tm]
    s4 = ahead(a4, POOL_HALO - 2)[:tm]
    s8 = ahead(a8, POOL_HALO - 4)[:tm]
    s16 = a16[:tm]
    grp = lax.broadcasted_iota(jnp.int32, (tm, POOL_W), 1) >> 6
    win_sum = jnp.where(grp == 0, s2, jnp.where(grp == 1, s4, jnp.where(grp == 2, s8, s16)))
    t = i * tm + lax.broadcasted_iota(jnp.int32, (tm, POOL_W), 0)
    half = jnp.left_shift(1, grp)
    cnt = jnp.minimum(t + half, seq_len) - jnp.maximum(t - half, 0)
    d = win_sum / cnt.astype(F32) - p
    pool = jnp.dot(d.astype(BF16), pw_ref[...], preferred_element_type=F32) * ps_ref[...]

    cat = jnp.concatenate([attn_ref[0], sgu.astype(BF16), pool.astype(BF16)], axis=-1)
    mix = jnp.dot(cat, wo_ref[...], preferred_element_type=F32)
    x1 = x_ref[0] + m[2:3] * mix

    h2 = (_rms(x1, n2_ref[...]) * (1.0 + m[4:5]) + m[3:4]).astype(BF16)
    gate = jnp.dot(h2, wg_ref[...], preferred_element_type=F32)
    up = jnp.dot(h2, wu_ref[...], preferred_element_type=F32)
    act = (_silu(gate) * up).astype(BF16)
    x2 = x1 + m[5:6] * jnp.dot(act, wd_ref[...], preferred_element_type=F32)
    if final:
        x2 = _rms(x2, fn_ref[...])
    o_ref[0] = x2


def _mixffn(x, mod_l, mod_row_of_batch, attn, u, vn, p, ws, bs, pw, ps, wo, n2, wg, wu, wd, fn,
            *, final):
    B, N, _ = x.shape
    tm = min(TOKEN_TILE, N)
    hb = tm // POOL_HALO
    n_hblocks = N // POOL_HALO
    const2 = lambda b, i: (0, 0)
    const3 = lambda b, i: (0, 0, 0)
    once = pl.Buffered(1)
    tile = lambda w: pl.BlockSpec((1, tm, w), lambda b, i: (b, i, 0))
    return pl.pallas_call(
        functools.partial(_mixffn_kernel, seq_len=N, final=final),
        grid=(B, N // tm),
        in_specs=[
            tile(D_MODEL),
            pl.BlockSpec((1, 6, D_MODEL), lambda b, i: (mod_row_of_batch(b), 0, 0)),
            tile(ATTN_W), tile(SGU_W), tile(SGU_W), tile(POOL_W),
            pl.BlockSpec((1, POOL_HALO, POOL_W), lambda b, i: (b, jnp.maximum(i * hb - 1, 0), 0)),
            pl.BlockSpec((1, POOL_HALO, POOL_W),
                         lambda b, i: (b, jnp.minimum((i + 1) * hb, n_hblocks - 1), 0)),
            pl.BlockSpec((SGU_GROUPS, CHUNK, CHUNK), const3, pipeline_mode=once),
            pl.BlockSpec((CHUNK, SGU_W), const2, pipeline_mode=once),
            pl.BlockSpec((POOL_W, POOL_W), const2, pipeline_mode=once),
            pl.BlockSpec((1, POOL_W), const2, pipeline_mode=once),
            pl.BlockSpec((MIX_W, D_MODEL), const2, pipeline_mode=once),
            pl.BlockSpec((1, D_MODEL), const2, pipeline_mode=once),
            pl.BlockSpec((D_MODEL, D_FF), const2, pipeline_mode=once),
            pl.BlockSpec((D_MODEL, D_FF), const2, pipeline_mode=once),
            pl.BlockSpec((D_FF, D_MODEL), const2, pipeline_mode=once),
            pl.BlockSpec((1, D_MODEL), const2, pipeline_mode=once),
        ],
        out_specs=tile(D_MODEL),
        out_shape=jax.ShapeDtypeStruct((B, N, D_MODEL), F32),
        compiler_params=pltpu.CompilerParams(
            dimension_semantics=("parallel", "parallel"), vmem_limit_bytes=VMEM_LIMIT),
        name="mixffn_final" if final else "mixffn",
    )(x, mod_l, attn, u, vn, p, p, p, ws, bs, pw, ps, wo, n2, wg, wu, wd, fn)


def _rope_tables(n_tokens):
    rows = jnp.repeat(jnp.arange(n_tokens // GRID_W, dtype=F32), GRID_W)
    cols = jnp.tile(jnp.arange(GRID_W, dtype=F32), n_tokens // GRID_W)
    inv = ROPE_THETA ** (-jnp.arange(ROPE_PAIRS_PER_AXIS, dtype=F32) / ROPE_PAIRS_PER_AXIS)
    ang = jnp.concatenate([rows[:, None] * inv, cols[:, None] * inv], axis=-1)
    cos, sin = jnp.cos(ang), jnp.sin(ang)
    return jnp.tile(cos, (1, 4)), jnp.tile(jnp.concatenate([-sin, sin], axis=-1), (1, 2))


def _block_diag(w):
    g, a, b = w.shape
    out = jnp.zeros((g * a, g * b), w.dtype)
    for k in range(g):
        out = out.at[k * a:(k + 1) * a, k * b:(k + 1) * b].set(w[k])
    return out


def kernel(x, c, ctx, c_ctx, w_mod, b_mod, norm1, norm2, w_in, q_norm, k_norm, sgu_norm, w_s, b_s,
           pool_w, pool_scale, w_out, w_gate, w_up, w_down, final_norm):
    B, S, _ = x.shape
    L = ctx.shape[1]
    c_rows = jnp.concatenate(
        [c, c_ctx[None, :], jnp.zeros((MOD_ROWS - B - 1, D_MODEL), F32)], axis=0)
    mod = _modulation(c_rows, w_mod, b_mod).reshape(DEPTH, MOD_ROWS, 6, D_MODEL)
    cos, sin = _rope_tables(S)
    ones = jnp.ones((L, LANES), F32)
    lat_row = lambda b: b
    ctx_row = lambda b: B

    head_order = jnp.array([4 * hh + s for s in range(ATTN_W // LANES) for hh in range(N_KV_HEADS)])
    q_cols = (head_order[:, None] * HEAD_DIM + jnp.arange(HEAD_DIM)[None, :]).reshape(-1)
    in_cols = jnp.concatenate([q_cols, jnp.arange(ATTN_W, IN_W)])
    out_rows = jnp.concatenate([q_cols, jnp.arange(ATTN_W, MIX_W)])

    for l in range(DEPTH):
        last = l == DEPTH - 1
        w_in_l = w_in[l][:, in_cols].astype(BF16)
        qg = jnp.tile(q_norm[l], 2)[None, :]
        kg = jnp.tile(k_norm[l], 2)[None, :]
        sg = sgu_norm[l][None, :]
        n1 = norm1[l][None, :]
        mix_w = (
            w_s[l].astype(BF16),
            jnp.repeat(b_s[l].T, SGU_W // SGU_GROUPS, axis=1),
            _block_diag(pool_w[l]).astype(BF16),
            pool_scale[l][None, :],
            w_out[l][out_rows].astype(BF16),
            norm2[l][None, :],
            w_gate[l].astype(BF16),
            w_up[l].astype(BF16),
            w_down[l].astype(BF16),
            final_norm[None, :],
        )

        q, k, vt, u, vn, p = _inproj(x, mod[l], lat_row, n1, w_in_l, qg, kg, sg, cos, sin,
                                     rope=True)
        qc, kc, vtc, uc, vnc, pc = _inproj(ctx, mod[l], ctx_row, n1, w_in_l, qg, kg, sg,
                                           ones, ones, rope=False)
        attn = _attention(q, [(kc, vtc), (k, vt)])
        if not last:
            attn_c = _attention(qc, [(kc, vtc)])
            ctx = _mixffn(ctx, mod[l], ctx_row, attn_c, uc, vnc, pc, *mix_w, final=False)
        x = _mixffn(x, mod[l], lat_row, attn, u, vn, p, *mix_w, final=last)
    return x
```

```python
import functools
import math

import jax
import jax.numpy as jnp
from jax import lax
from jax.experimental import pallas as pl
from jax.experimental.pallas import tpu as pltpu

F32 = jnp.float32
BF16 = jnp.bfloat16

D_MODEL = 1024
DEPTH = 2
GRID_W = 64
EPS = 1e-6

HEAD_DIM = 64
N_HEADS = 8
N_KV_HEADS = 2
Q_PER_KV = N_HEADS // N_KV_HEADS
ATTN_W = N_HEADS * HEAD_DIM
KV_W = N_KV_HEADS * HEAD_DIM
ROPE_THETA = 10000.0
ROPE_PAIRS_PER_AXIS = HEAD_DIM // 4
ATTN_SCALE = HEAD_DIM ** -0.5
LOG2E = math.log2(math.e)

CHUNK = 128
SGU_W = D_MODEL // 4
SGU_GROUPS = 4
POOL_WINDOWS = (2, 4, 8, 16)
POOL_W = D_MODEL // 4
POOL_HALO = 8

MIX_W = ATTN_W + SGU_W + POOL_W
O_Q = 0
O_K = O_Q + ATTN_W
O_V = O_K + KV_W
O_G = O_V + KV_W
O_P = O_G + 2 * SGU_W
IN_W = O_P + POOL_W
D_FF = 2816

LANES = 128
MOD_ROWS = 16
VMEM_LIMIT = 56 * 1024 * 1024

TOKEN_TILE = 256
INPROJ_TILE = 1024
SUB_TILE = 256
Q_TILE = 512


def _half_mask(shape):
    lane = lax.broadcasted_iota(jnp.int32, shape, len(shape) - 1)
    return (lane & HEAD_DIM) == 0


def _rms(x, gain):
    return x * lax.rsqrt(jnp.mean(x * x, axis=-1, keepdims=True) + EPS) * gain


def _head_rms_slab(z, gain):
    lo = _half_mask(z.shape)
    sq = z * z
    s0 = jnp.sum(jnp.where(lo, sq, 0.0), axis=-1, keepdims=True)
    s1 = jnp.sum(jnp.where(lo, 0.0, sq), axis=-1, keepdims=True)
    ms = jnp.where(lo, s0, s1) * (1.0 / HEAD_DIM)
    return z * lax.rsqrt(ms + EPS) * gain


def _rope_slab(x, cos, sin_signed):
    lane = lax.broadcasted_iota(jnp.int32, x.shape, 1)
    first = (lane & (HEAD_DIM // 2)) == 0
    up = pltpu.roll(x, LANES - HEAD_DIM // 2, axis=1)
    down = pltpu.roll(x, HEAD_DIM // 2, axis=1)
    partner = jnp.where(first, up, down)
    return x * cos + partner * sin_signed


def _gelu_tanh(x):
    c = math.sqrt(2.0 / math.pi)
    return 0.5 * x * (1.0 + jnp.tanh(c * (x + 0.044715 * (x * x * x))))


def _silu(x):
    return x / (1.0 + jnp.exp(-x))


def _mod_kernel(c_ref, w_ref, b_ref, o_ref):
    a = _silu(c_ref[...])
    o_ref[0] = jnp.dot(a, w_ref[0], preferred_element_type=F32,
                       precision=lax.Precision.HIGHEST) + b_ref[0]


def _modulation(c_rows, w_mod, b_mod):
    n_col = 6
    return pl.pallas_call(
        _mod_kernel,
        grid=(DEPTH, n_col),
        in_specs=[
            pl.BlockSpec((MOD_ROWS, D_MODEL), lambda l, j: (0, 0)),
            pl.BlockSpec((1, D_MODEL, D_MODEL), lambda l, j: (l, 0, j)),
            pl.BlockSpec((1, 1, D_MODEL), lambda l, j: (l, 0, j)),
        ],
        out_specs=pl.BlockSpec((1, MOD_ROWS, D_MODEL), lambda l, j: (l, 0, j)),
        out_shape=jax.ShapeDtypeStruct((DEPTH, MOD_ROWS, 6 * D_MODEL), F32),
        compiler_params=pltpu.CompilerParams(dimension_semantics=("parallel", "parallel")),
        name="modulation",
    )(c_rows, w_mod, b_mod.reshape(DEPTH, 1, 6 * D_MODEL))


def _inproj_kernel(x_ref, mod_ref, n1_ref, w_ref, qg_ref, kg_ref, sg_ref, cos_ref, sin_ref,
                   q_ref, k_ref, vt_ref, u_ref, vn_ref, p_ref, *, rope):
    m = mod_ref[0]
    tm = x_ref.shape[1]
    sub = min(SUB_TILE, tm)

    def project(r0):
        rows = slice(r0, r0 + sub)
        h = _rms(x_ref[0, rows, :], n1_ref[...]) * (1.0 + m[1:2]) + m[0:1]
        return jnp.dot(h.astype(BF16), w_ref[...], preferred_element_type=F32)

    def finish(r0, z):
        rows = slice(r0, r0 + sub)
        if rope:
            cos = cos_ref[rows, :]
            sin = sin_ref[rows, :]

        for j in range(ATTN_W // LANES):
            zq = z[:, O_Q + j * LANES:O_Q + (j + 1) * LANES]
            qn = _head_rms_slab(zq, qg_ref[...])
            if rope:
                qn = _rope_slab(qn, cos, sin)
            q_ref[0, rows, j * LANES:(j + 1) * LANES] = (qn * (ATTN_SCALE * LOG2E)).astype(BF16)

        kn = _head_rms_slab(z[:, O_K:O_K + KV_W], kg_ref[...])
        if rope:
            kn = _rope_slab(kn, cos, sin)
        k_ref[0, rows, :] = kn.astype(BF16)

        vt = z[:, O_V:O_V + KV_W].T
        ones = jnp.ones((HEAD_DIM, sub), F32)
        for hh in range(N_KV_HEADS):
            vt_ref[0, hh, :, rows] = jnp.concatenate(
                [vt[hh * HEAD_DIM:(hh + 1) * HEAD_DIM], ones], axis=0).astype(BF16)

        g = _gelu_tanh(z[:, O_G:O_P])
        u_ref[0, rows, :] = g[:, :SGU_W]
        vn_ref[0, rows, :] = _rms(g[:, SGU_W:], sg_ref[...]).astype(BF16)

        p_ref[0, rows, :] = z[:, O_P:]

    starts = list(range(0, tm, sub))
    z = project(starts[0])
    for r0, r1 in zip(starts, starts[1:] + [None]):
        z_next = project(r1) if r1 is not None else None
        finish(r0, z)
        z = z_next


def _inproj(x, mod, mod_row_of_batch, norm1, w_in, qg, kg, sg, cos, sin, *, rope):
    B, N, _ = x.shape
    tm = min(INPROJ_TILE, N)
    grid = (B, N // tm)
    const = lambda b, i: (0, 0)
    return pl.pallas_call(
        functools.partial(_inproj_kernel, rope=rope),
        grid=grid,
        in_specs=[
            pl.BlockSpec((1, tm, D_MODEL), lambda b, i: (b, i, 0)),
            pl.BlockSpec((None, 1, 6, D_MODEL), lambda b, i: (*mod_row_of_batch(b), 0, 0)),
            pl.BlockSpec((1, D_MODEL), const),
            pl.BlockSpec((D_MODEL, IN_W), const),
            pl.BlockSpec((1, LANES), const),
            pl.BlockSpec((1, LANES), const),
            pl.BlockSpec((1, SGU_W), const),
            pl.BlockSpec((tm, LANES), lambda b, i: (i, 0)),
            pl.BlockSpec((tm, LANES), lambda b, i: (i, 0)),
        ],
        out_specs=[
            pl.BlockSpec((1, tm, ATTN_W), lambda b, i: (b, i, 0)),
            pl.BlockSpec((1, tm, KV_W), lambda b, i: (b, i, 0)),
            pl.BlockSpec((1, N_KV_HEADS, 2 * HEAD_DIM, tm), lambda b, i: (b, 0, 0, i)),
            pl.BlockSpec((1, tm, SGU_W), lambda b, i: (b, i, 0)),
            pl.BlockSpec((1, tm, SGU_W), lambda b, i: (b, i, 0)),
            pl.BlockSpec((1, tm, POOL_W), lambda b, i: (b, i, 0)),
        ],
        out_shape=[
            jax.ShapeDtypeStruct((B, N, ATTN_W), BF16),
            jax.ShapeDtypeStruct((B, N, KV_W), BF16),
            jax.ShapeDtypeStruct((B, N_KV_HEADS, 2 * HEAD_DIM, N), BF16),
            jax.ShapeDtypeStruct((B, N, SGU_W), F32),
            jax.ShapeDtypeStruct((B, N, SGU_W), BF16),
            jax.ShapeDtypeStruct((B, N, POOL_W), F32),
        ],
        compiler_params=pltpu.CompilerParams(
            dimension_semantics=("parallel", "parallel"), vmem_limit_bytes=VMEM_LIMIT),
        name="inproj_rope" if rope else "inproj",
    )(x, mod, norm1, w_in, qg, kg, sg, cos, sin)


def _attn_kernel(*refs, n_kv_sets):
    q_ref = refs[0]
    k_refs = refs[1:1 + n_kv_sets]
    vt_refs = refs[1 + n_kv_sets:1 + 2 * n_kv_sets]
    o_ref = refs[1 + 2 * n_kv_sets]
    tq = q_ref.shape[1]
    lo = _half_mask((tq, LANES))
    zero = jnp.zeros((tq, LANES), BF16)
    nt = (((1,), (1,)), ((), ()))

    def scores(idx):
        s, hh = idx // N_KV_HEADS, idx % N_KV_HEADS
        qs = q_ref[0, :, s * LANES:(s + 1) * LANES]
        qm = jnp.where(lo, qs, zero) if hh == 0 else jnp.where(lo, zero, qs)
        st = [lax.dot_general(k[0], qm, nt, preferred_element_type=F32) for k in k_refs]
        mx = functools.reduce(jnp.maximum, [jnp.max(si, axis=0, keepdims=True) for si in st])
        return st, mx

    def weighted_values(idx, st, mx):
        hh = idx % N_KV_HEADS
        acc = functools.reduce(lambda a, b: a + b, [
            jnp.dot(vt[0, hh], jnp.exp2(si - mx).astype(BF16), preferred_element_type=F32)
            for si, vt in zip(st, vt_refs)])
        return acc[:HEAD_DIM] / acc[HEAD_DIM:HEAD_DIM + 1]

    heads = []
    pending = scores(0)
    for idx in range(N_HEADS):
        nxt = scores(idx + 1) if idx + 1 < N_HEADS else None
        heads.append(weighted_values(idx, *pending))
        pending = nxt
    o_ref[0] = jnp.concatenate(heads, axis=0).T.astype(BF16)


def _attention(q, kv_sets):
    B, N, _ = q.shape
    tq = min(Q_TILE, N)
    in_specs = [pl.BlockSpec((1, tq, ATTN_W), lambda b, i: (b, i, 0))]
    for k, _ in kv_sets:
        in_specs.append(pl.BlockSpec((1, k.shape[1], KV_W), lambda b, i: (b, 0, 0)))
    for _, vt in kv_sets:
        in_specs.append(pl.BlockSpec((1, N_KV_HEADS, 2 * HEAD_DIM, vt.shape[3]),
                                     lambda b, i: (b, 0, 0, 0)))
    return pl.pallas_call(
        functools.partial(_attn_kernel, n_kv_sets=len(kv_sets)),
        grid=(B, N // tq),
        in_specs=in_specs,
        out_specs=pl.BlockSpec((1, tq, ATTN_W), lambda b, i: (b, i, 0)),
        out_shape=jax.ShapeDtypeStruct((B, N, ATTN_W), BF16),
        compiler_params=pltpu.CompilerParams(
            dimension_semantics=("parallel", "parallel"), vmem_limit_bytes=VMEM_LIMIT),
        name=f"attention_{len(kv_sets)}",
    )(q, *[k for k, _ in kv_sets], *[vt for _, vt in kv_sets])


def _mix_stage(i, n_tiles, x_ref, mod_ref, attn_ref, u_ref, vn_ref, p_ref, pprev_ref, pnext_ref,
               ws_ref, bs_ref, pw_ref, ps_ref, wo_ref, n2_ref, seq_len):
    tm = x_ref.shape[1]
    m = mod_ref[0]

    lo = _half_mask((CHUNK, LANES))
    sgu_chunks = []
    for c in range(tm // CHUNK):
        slabs = []
        for j in range(SGU_W // LANES):
            vs = vn_ref[0, c * CHUNK:(c + 1) * CHUNK, j * LANES:(j + 1) * LANES]
            ra = jnp.dot(ws_ref[2 * j], vs, preferred_element_type=F32)
            rb = jnp.dot(ws_ref[2 * j + 1], vs, preferred_element_type=F32)
            slabs.append(jnp.where(lo, ra, rb))
        mixed = jnp.concatenate(slabs, axis=-1) + bs_ref[...]
        sgu_chunks.append(u_ref[0, c * CHUNK:(c + 1) * CHUNK, :] * mixed)
    sgu = jnp.concatenate(sgu_chunks, axis=0)

    p = p_ref[0]
    prev = jnp.where(i > 0, pprev_ref[0], 0.0)
    nxt = jnp.where(i < n_tiles - 1, pnext_ref[0], 0.0)
    e = jnp.concatenate([prev, p, nxt], axis=0)
    n_ext = tm + 2 * POOL_HALO

    def ahead(a, k):
        return pltpu.roll(a, n_ext - k, axis=0)

    a2 = e + ahead(e, 1)
    a4 = a2 + ahead(a2, 2)
    a8 = a4 + ahead(a4, 4)
    a16 = a8 + ahead(a8, 8)
    s2 = ahead(a2, POOL_HALO - 1)[:tm]
    s4 = ahead(a4, POOL_HALO - 2)[:tm]
    s8 = ahead(a8, POOL_HALO - 4)[:tm]
    s16 = a16[:tm]
    grp = lax.broadcasted_iota(jnp.int32, (tm, POOL_W), 1) >> 6
    win_sum = jnp.where(grp == 0, s2, jnp.where(grp == 1, s4, jnp.where(grp == 2, s8, s16)))
    t = i * tm + lax.broadcasted_iota(jnp.int32, (tm, POOL_W), 0)
    half = jnp.left_shift(1, grp)
    cnt = jnp.minimum(t + half, seq_len) - jnp.maximum(t - half, 0)
    d = win_sum / cnt.astype(F32) - p
    pool = jnp.dot(d.astype(BF16), pw_ref[...], preferred_element_type=F32) * ps_ref[...]

    cat = jnp.concatenate([attn_ref[0], sgu.astype(BF16), pool.astype(BF16)], axis=-1)
    mix = jnp.dot(cat, wo_ref[...], preferred_element_type=F32)
    x1 = x_ref[0] + m[2:3] * mix
    h2 = (_rms(x1, n2_ref[...]) * (1.0 + m[4:5]) + m[3:4]).astype(BF16)
    return x1, h2


def _ffn_stage(x1, h2, gate2, wg_ref, wu_ref, wd_ref, fn_ref, final):
    gate = jnp.dot(h2, wg_ref[...], preferred_element_type=F32)
    up = jnp.dot(h2, wu_ref[...], preferred_element_type=F32)
    act = (_silu(gate) * up).astype(BF16)
    x2 = x1 + gate2 * jnp.dot(act, wd_ref[...], preferred_element_type=F32)
    if final:
        x2 = _rms(x2, fn_ref[...])
    return x2


def _mixffn_kernel(x_ref, mod_ref, modp_ref, attn_ref, u_ref, vn_ref, p_ref, pprev_ref, pnext_ref,
                   ws_ref, bs_ref, pw_ref, ps_ref, wo_ref, n2_ref, wg_ref, wu_ref, wd_ref,
                   fn_ref, o_ref, x1_scr, h2_scr, *, seq_len, tiles_per_seq, final):
    s = pl.program_id(0)
    n_steps = pl.num_programs(0)
    i = lax.rem(jnp.minimum(s, n_steps - 2), tiles_per_seq)
    mix_args = (i, tiles_per_seq, x_ref, mod_ref, attn_ref, u_ref, vn_ref, p_ref, pprev_ref,
                pnext_ref, ws_ref, bs_ref, pw_ref, ps_ref, wo_ref, n2_ref, seq_len)
    ffn_args = (wg_ref, wu_ref, wd_ref, fn_ref, final)

    @pl.when(s == 0)
    def _():
        x1, h2 = _mix_stage(*mix_args)
        x1_scr[...] = x1
        h2_scr[...] = h2

    @pl.when(jnp.logical_and(s > 0, s < n_steps - 1))
    def _():
        x1_prev = x1_scr[...]
        h2_prev = h2_scr[...]
        o_ref[0] = _ffn_stage(x1_prev, h2_prev, modp_ref[0][5:6], *ffn_args)
        x1, h2 = _mix_stage(*mix_args)
        x1_scr[...] = x1
        h2_scr[...] = h2

    @pl.when(s == n_steps - 1)
    def _():
        o_ref[0] = _ffn_stage(x1_scr[...], h2_scr[...], modp_ref[0][5:6], *ffn_args)


def _mixffn(x, mod, mod_row_of_batch, attn, u, vn, p, ws, bs, pw, ps, wo, n2, wg, wu, wd, fn,
            *, final):
    B, N, _ = x.shape
    tm = min(TOKEN_TILE, N)
    tps = N // tm
    n_tiles = B * tps
    hb = tm // POOL_HALO
    n_hblocks = N // POOL_HALO

    def cur(s):
        t = jnp.minimum(s, n_tiles - 1)
        return t // tps, lax.rem(t, tps)

    def prv(s):
        t = jnp.maximum(s - 1, 0)
        return t // tps, lax.rem(t, tps)

    def tile(w):
        return pl.BlockSpec((1, tm, w), lambda s: (*cur(s), 0))

    def halo_prev(s):
        b, i = cur(s)
        return b, jnp.maximum(i * hb - 1, 0), 0

    def halo_next(s):
        b, i = cur(s)
        return b, jnp.minimum((i + 1) * hb, n_hblocks - 1), 0

    const2 = lambda s: (0, 0)
    const3 = lambda s: (0, 0, 0)
    once = pl.Buffered(1)
    return pl.pallas_call(
        functools.partial(_mixffn_kernel, seq_len=N, tiles_per_seq=tps, final=final),
        grid=(n_tiles + 1,),
        in_specs=[
            tile(D_MODEL),
            pl.BlockSpec((None, 1, 6, D_MODEL), lambda s: (*mod_row_of_batch(cur(s)[0]), 0, 0)),
            pl.BlockSpec((None, 1, 6, D_MODEL), lambda s: (*mod_row_of_batch(prv(s)[0]), 0, 0)),
            tile(ATTN_W), tile(SGU_W), tile(SGU_W), tile(POOL_W),
            pl.BlockSpec((1, POOL_HALO, POOL_W), halo_prev),
            pl.BlockSpec((1, POOL_HALO, POOL_W), halo_next),
            pl.BlockSpec((SGU_GROUPS, CHUNK, CHUNK), const3, pipeline_mode=once),
            pl.BlockSpec((CHUNK, SGU_W), const2, pipeline_mode=once),
            pl.BlockSpec((POOL_W, POOL_W), const2, pipeline_mode=once),
            pl.BlockSpec((1, POOL_W), const2, pipeline_mode=once),
            pl.BlockSpec((MIX_W, D_MODEL), const2, pipeline_mode=once),
            pl.BlockSpec((1, D_MODEL), const2, pipeline_mode=once),
            pl.BlockSpec((D_MODEL, D_FF), const2, pipeline_mode=once),
            pl.BlockSpec((D_MODEL, D_FF), const2, pipeline_mode=once),
            pl.BlockSpec((D_FF, D_MODEL), const2, pipeline_mode=once),
            pl.BlockSpec((1, D_MODEL), const2, pipeline_mode=once),
        ],
        out_specs=pl.BlockSpec((1, tm, D_MODEL), lambda s: (*prv(s), 0)),
        out_shape=jax.ShapeDtypeStruct((B, N, D_MODEL), F32),
        scratch_shapes=[pltpu.VMEM((tm, D_MODEL), F32), pltpu.VMEM((tm, D_MODEL), BF16)],
        compiler_params=pltpu.CompilerParams(
            dimension_semantics=("arbitrary",), vmem_limit_bytes=VMEM_LIMIT),
        name="mixffn_final" if final else "mixffn",
    )(x, mod, mod, attn, u, vn, p, p, p, ws, bs, pw, ps, wo, n2, wg, wu, wd, fn)


def _rope_tables(n_tokens):
    rows = jnp.repeat(jnp.arange(n_tokens // GRID_W, dtype=F32), GRID_W)
    cols = jnp.tile(jnp.arange(GRID_W, dtype=F32), n_tokens // GRID_W)
    inv = ROPE_THETA ** (-jnp.arange(ROPE_PAIRS_PER_AXIS, dtype=F32) / ROPE_PAIRS_PER_AXIS)
    ang = jnp.concatenate([rows[:, None] * inv, cols[:, None] * inv], axis=-1)
    cos, sin = jnp.cos(ang), jnp.sin(ang)
    return jnp.tile(cos, (1, 4)), jnp.tile(jnp.concatenate([-sin, sin], axis=-1), (1, 2))


def _block_diag(w):
    g, a, b = w.shape
    out = jnp.zeros((g * a, g * b), w.dtype)
    for k in range(g):
        out = out.at[k * a:(k + 1) * a, k * b:(k + 1) * b].set(w[k])
    return out


def kernel(x, c, ctx, c_ctx, w_mod, b_mod, norm1, norm2, w_in, q_norm, k_norm, sgu_norm, w_s, b_s,
           pool_w, pool_scale, w_out, w_gate, w_up, w_down, final_norm):
    B, S, _ = x.shape
    L = ctx.shape[1]
    c_rows = jnp.concatenate(
        [c, c_ctx[None, :], jnp.zeros((MOD_ROWS - B - 1, D_MODEL), F32)], axis=0)
    mod = _modulation(c_rows, w_mod, b_mod).reshape(DEPTH, MOD_ROWS, 6, D_MODEL)
    cos, sin = _rope_tables(S)
    ones = jnp.ones((L, LANES), F32)

    head_order = jnp.array([4 * hh + s for s in range(ATTN_W // LANES) for hh in range(N_KV_HEADS)])
    q_cols = (head_order[:, None] * HEAD_DIM + jnp.arange(HEAD_DIM)[None, :]).reshape(-1)
    in_cols = jnp.concatenate([q_cols, jnp.arange(ATTN_W, IN_W)])
    out_rows = jnp.concatenate([q_cols, jnp.arange(ATTN_W, MIX_W)])

    for l in range(DEPTH):
        last = l == DEPTH - 1
        w_in_l = w_in[l][:, in_cols].astype(BF16)
        qg = jnp.tile(q_norm[l], 2)[None, :]
        kg = jnp.tile(k_norm[l], 2)[None, :]
        sg = sgu_norm[l][None, :]
        n1 = norm1[l][None, :]
        mix_w = (
            w_s[l].astype(BF16),
            jnp.repeat(b_s[l].T, SGU_W // SGU_GROUPS, axis=1),
            _block_diag(pool_w[l]).astype(BF16),
            pool_scale[l][None, :],
            w_out[l][out_rows].astype(BF16),
            norm2[l][None, :],
            w_gate[l].astype(BF16),
            w_up[l].astype(BF16),
            w_down[l].astype(BF16),
            final_norm[None, :],
        )

        lat_row = lambda b, l=l: (l, b)
        ctx_row = lambda b, l=l: (l, B)
        q, k, vt, u, vn, p = _inproj(x, mod, lat_row, n1, w_in_l, qg, kg, sg, cos, sin, rope=True)
        qc, kc, vtc, uc, vnc, pc = _inproj(ctx, mod, ctx_row, n1, w_in_l, qg, kg, sg,
                                           ones, ones, rope=False)
        attn = _attention(q, [(kc, vtc), (k, vt)])
        if not last:
            attn_c = _attention(qc, [(kc, vtc)])
            ctx = _mixffn(ctx, mod, ctx_row, attn_c, uc, vnc, pc, *mix_w, final=False)
        x = _mixffn(x, mod, lat_row, attn, u, vn, p, *mix_w, final=last)
    return x
```

```python
import functools
import math

import jax
import jax.numpy as jnp
from jax import lax
from jax.experimental import pallas as pl
from jax.experimental.pallas import tpu as pltpu

F32 = jnp.float32
BF16 = jnp.bfloat16

D_MODEL = 1024
DEPTH = 2
GRID_W = 64
EPS = 1e-6

HEAD_DIM = 64
N_HEADS = 8
N_KV_HEADS = 2
Q_PER_KV = N_HEADS // N_KV_HEADS
ATTN_W = N_HEADS * HEAD_DIM
KV_W = N_KV_HEADS * HEAD_DIM
ROPE_THETA = 10000.0
ROPE_PAIRS_PER_AXIS = HEAD_DIM // 4
ATTN_SCALE = HEAD_DIM ** -0.5
LOG2E = math.log2(math.e)

CHUNK = 128
SGU_W = D_MODEL // 4
SGU_GROUPS = 4
POOL_WINDOWS = (2, 4, 8, 16)
POOL_W = D_MODEL // 4
POOL_HALO = 8

MIX_W = ATTN_W + SGU_W + POOL_W
O_Q = 0
O_K = O_Q + ATTN_W
O_V = O_K + KV_W
O_G = O_V + KV_W
O_P = O_G + 2 * SGU_W
IN_W = O_P + POOL_W
D_FF = 2816

LANES = 128
MOD_ROWS = 16
VMEM_LIMIT = 56 * 1024 * 1024

TOKEN_TILE = 256
INPROJ_TILE = 1024
SUB_TILE = 256
Q_TILE = 512


def _half_mask(shape):
    lane = lax.broadcasted_iota(jnp.int32, shape, len(shape) - 1)
    return (lane & HEAD_DIM) == 0


def _rms(x, gain):
    return x * lax.rsqrt(jnp.mean(x * x, axis=-1, keepdims=True) + EPS) * gain


def _head_rms_slab(z, gain):
    lo = _half_mask(z.shape)
    sq = z * z
    s0 = jnp.sum(jnp.where(lo, sq, 0.0), axis=-1, keepdims=True)
    s1 = jnp.sum(jnp.where(lo, 0.0, sq), axis=-1, keepdims=True)
    ms = jnp.where(lo, s0, s1) * (1.0 / HEAD_DIM)
    return z * lax.rsqrt(ms + EPS) * gain


def _rope_slab(x, cos, sin_signed):
    lane = lax.broadcasted_iota(jnp.int32, x.shape, 1)
    first = (lane & (HEAD_DIM // 2)) == 0
    up = pltpu.roll(x, LANES - HEAD_DIM // 2, axis=1)
    down = pltpu.roll(x, HEAD_DIM // 2, axis=1)
    partner = jnp.where(first, up, down)
    return x * cos + partner * sin_signed


def _gelu_tanh(x):
    c = math.sqrt(2.0 / math.pi)
    return 0.5 * x * (1.0 + jnp.tanh(c * (x + 0.044715 * (x * x * x))))


def _silu(x):
    return x / (1.0 + jnp.exp(-x))


def _mod_kernel(c_ref, w_ref, b_ref, o_ref):
    a = _silu(c_ref[...])
    w = w_ref[0]
    a_hi = a.astype(BF16)
    a_lo = (a - a_hi.astype(F32)).astype(BF16)
    w_hi = w.astype(BF16)
    w_lo = (w - w_hi.astype(F32)).astype(BF16)
    dot = functools.partial(jnp.dot, preferred_element_type=F32)
    o_ref[0] = dot(a_hi, w_hi) + (dot(a_lo, w_hi) + dot(a_hi, w_lo)) + b_ref[0]


def _modulation(c_rows, w_mod, b_mod):
    n_col = 6
    return pl.pallas_call(
        _mod_kernel,
        grid=(DEPTH, n_col),
        in_specs=[
            pl.BlockSpec((MOD_ROWS, D_MODEL), lambda l, j: (0, 0)),
            pl.BlockSpec((1, D_MODEL, D_MODEL), lambda l, j: (l, 0, j)),
            pl.BlockSpec((1, 1, D_MODEL), lambda l, j: (l, 0, j)),
        ],
        out_specs=pl.BlockSpec((1, MOD_ROWS, D_MODEL), lambda l, j: (l, 0, j)),
        out_shape=jax.ShapeDtypeStruct((DEPTH, MOD_ROWS, 6 * D_MODEL), F32),
        compiler_params=pltpu.CompilerParams(dimension_semantics=("parallel", "parallel")),
        name="modulation",
    )(c_rows, w_mod, b_mod.reshape(DEPTH, 1, 6 * D_MODEL))


def _inproj_kernel(x_ref, mod_ref, n1_ref, w_ref, qg_ref, kg_ref, sg_ref, cos_ref, sin_ref,
                   q_ref, k_ref, vt_ref, u_ref, vn_ref, p_ref, *, rope):
    m = mod_ref[0]
    tm = x_ref.shape[1]
    sub = min(SUB_TILE, tm)

    def project(r0):
        rows = slice(r0, r0 + sub)
        h = _rms(x_ref[0, rows, :], n1_ref[...]) * (1.0 + m[1:2]) + m[0:1]
        return jnp.dot(h.astype(BF16), w_ref[...], preferred_element_type=F32)

    def finish(r0, z):
        rows = slice(r0, r0 + sub)
        if rope:
            cos = cos_ref[rows, :]
            sin = sin_ref[rows, :]

        for j in range(ATTN_W // LANES):
            zq = z[:, O_Q + j * LANES:O_Q + (j + 1) * LANES]
            qn = _head_rms_slab(zq, qg_ref[...])
            if rope:
                qn = _rope_slab(qn, cos, sin)
            q_ref[0, rows, j * LANES:(j + 1) * LANES] = (qn * (ATTN_SCALE * LOG2E)).astype(BF16)

        kn = _head_rms_slab(z[:, O_K:O_K + KV_W], kg_ref[...])
        if rope:
            kn = _rope_slab(kn, cos, sin)
        k_ref[0, rows, :] = kn.astype(BF16)

        vt = z[:, O_V:O_V + KV_W].T
        ones = jnp.ones((HEAD_DIM, sub), F32)
        for hh in range(N_KV_HEADS):
            vt_ref[0, hh, :, rows] = jnp.concatenate(
                [vt[hh * HEAD_DIM:(hh + 1) * HEAD_DIM], ones], axis=0).astype(BF16)

        g = _gelu_tanh(z[:, O_G:O_P])
        u_ref[0, rows, :] = g[:, :SGU_W]
        vn_ref[0, rows, :] = _rms(g[:, SGU_W:], sg_ref[...]).astype(BF16)

        p_ref[0, rows, :] = z[:, O_P:]

    starts = list(range(0, tm, sub))
    z = project(starts[0])
    for r0, r1 in zip(starts, starts[1:] + [None]):
        z_next = project(r1) if r1 is not None else None
        finish(r0, z)
        z = z_next


def _layer_spec(layer, *shape, **kwargs):
    return pl.BlockSpec((None, *shape), lambda *_: (layer,) + (0,) * len(shape), **kwargs)


def _inproj(layer, x, mod, mod_row_of_batch, norm1, w_in, qg, kg, sg, cos, sin, *, rope):
    B, N, _ = x.shape
    tm = min(INPROJ_TILE, N)
    grid = (B, N // tm)
    return pl.pallas_call(
        functools.partial(_inproj_kernel, rope=rope),
        grid=grid,
        in_specs=[
            pl.BlockSpec((1, tm, D_MODEL), lambda b, i: (b, i, 0)),
            pl.BlockSpec((None, 1, 6, D_MODEL), lambda b, i: (layer, mod_row_of_batch(b), 0, 0)),
            _layer_spec(layer, 1, D_MODEL),
            _layer_spec(layer, D_MODEL, IN_W),
            _layer_spec(layer, 1, LANES),
            _layer_spec(layer, 1, LANES),
            _layer_spec(layer, 1, SGU_W),
            pl.BlockSpec((tm, LANES), lambda b, i: (i, 0)),
            pl.BlockSpec((tm, LANES), lambda b, i: (i, 0)),
        ],
        out_specs=[
            pl.BlockSpec((1, tm, ATTN_W), lambda b, i: (b, i, 0)),
            pl.BlockSpec((1, tm, KV_W), lambda b, i: (b, i, 0)),
            pl.BlockSpec((1, N_KV_HEADS, 2 * HEAD_DIM, tm), lambda b, i: (b, 0, 0, i)),
            pl.BlockSpec((1, tm, SGU_W), lambda b, i: (b, i, 0)),
            pl.BlockSpec((1, tm, SGU_W), lambda b, i: (b, i, 0)),
            pl.BlockSpec((1, tm, POOL_W), lambda b, i: (b, i, 0)),
        ],
        out_shape=[
            jax.ShapeDtypeStruct((B, N, ATTN_W), BF16),
            jax.ShapeDtypeStruct((B, N, KV_W), BF16),
            jax.ShapeDtypeStruct((B, N_KV_HEADS, 2 * HEAD_DIM, N), BF16),
            jax.ShapeDtypeStruct((B, N, SGU_W), F32),
            jax.ShapeDtypeStruct((B, N, SGU_W), BF16),
            jax.ShapeDtypeStruct((B, N, POOL_W), F32),
        ],
        compiler_params=pltpu.CompilerParams(
            dimension_semantics=("parallel", "parallel"), vmem_limit_bytes=VMEM_LIMIT),
        name="inproj_rope" if rope else "inproj",
    )(x, mod, norm1, w_in, qg, kg, sg, cos, sin)


def _attn_kernel(*refs, n_kv_sets):
    q_ref = refs[0]
    k_refs = refs[1:1 + n_kv_sets]
    vt_refs = refs[1 + n_kv_sets:1 + 2 * n_kv_sets]
    o_ref = refs[1 + 2 * n_kv_sets]
    tq = q_ref.shape[1]
    lo = _half_mask((tq, LANES))
    zero = jnp.zeros((tq, LANES), BF16)
    nt = (((1,), (1,)), ((), ()))

    def scores(idx):
        s, hh = idx // N_KV_HEADS, idx % N_KV_HEADS
        qs = q_ref[0, :, s * LANES:(s + 1) * LANES]
        qm = jnp.where(lo, qs, zero) if hh == 0 else jnp.where(lo, zero, qs)
        st = [lax.dot_general(k[0], qm, nt, preferred_element_type=F32) for k in k_refs]
        mx = functools.reduce(jnp.maximum, [jnp.max(si, axis=0, keepdims=True) for si in st])
        return st, mx

    def weighted_values(idx, st, mx):
        hh = idx % N_KV_HEADS
        acc = functools.reduce(lambda a, b: a + b, [
            jnp.dot(vt[0, hh], jnp.exp2(si - mx).astype(BF16), preferred_element_type=F32)
            for si, vt in zip(st, vt_refs)])
        return acc[:HEAD_DIM] / acc[HEAD_DIM:HEAD_DIM + 1]

    heads = []
    pending = scores(0)
    for idx in range(N_HEADS):
        nxt = scores(idx + 1) if idx + 1 < N_HEADS else None
        heads.append(weighted_values(idx, *pending))
        pending = nxt
    o_ref[0] = jnp.concatenate(heads, axis=0).T.astype(BF16)


def _attention(q, kv_sets):
    B, N, _ = q.shape
    tq = min(Q_TILE, N)
    in_specs = [pl.BlockSpec((1, tq, ATTN_W), lambda b, i: (b, i, 0))]
    for k, _ in kv_sets:
        in_specs.append(pl.BlockSpec((1, k.shape[1], KV_W), lambda b, i: (b, 0, 0)))
    for _, vt in kv_sets:
        in_specs.append(pl.BlockSpec((1, N_KV_HEADS, 2 * HEAD_DIM, vt.shape[3]),
                                     lambda b, i: (b, 0, 0, 0)))
    return pl.pallas_call(
        functools.partial(_attn_kernel, n_kv_sets=len(kv_sets)),
        grid=(B, N // tq),
        in_specs=in_specs,
        out_specs=pl.BlockSpec((1, tq, ATTN_W), lambda b, i: (b, i, 0)),
        out_shape=jax.ShapeDtypeStruct((B, N, ATTN_W), BF16),
        compiler_params=pltpu.CompilerParams(
            dimension_semantics=("parallel", "parallel"), vmem_limit_bytes=VMEM_LIMIT),
        name=f"attention_{len(kv_sets)}",
    )(q, *[k for k, _ in kv_sets], *[vt for _, vt in kv_sets])


def _mix_stage(i, n_tiles, x_ref, mod_ref, attn_ref, u_ref, vn_ref, p_ref, pprev_ref, pnext_ref,
               ws_ref, bs_ref, pw_ref, ps_ref, wo_ref, n2_ref, seq_len):
    tm = x_ref.shape[1]
    m = mod_ref[0]

    lo = _half_mask((CHUNK, LANES))
    sgu_chunks = []
    for c in range(tm // CHUNK):
        slabs = []
        for j in range(SGU_W // LANES):
            vs = vn_ref[0, c * CHUNK:(c + 1) * CHUNK, j * LANES:(j + 1) * LANES]
            ra = jnp.dot(ws_ref[2 * j], vs, preferred_element_type=F32)
            rb = jnp.dot(ws_ref[2 * j + 1], vs, preferred_element_type=F32)
            slabs.append(jnp.where(lo, ra, rb))
        mixed = jnp.concatenate(slabs, axis=-1) + bs_ref[...]
        sgu_chunks.append(u_ref[0, c * CHUNK:(c + 1) * CHUNK, :] * mixed)
    sgu = jnp.concatenate(sgu_chunks, axis=0)

    p = p_ref[0]
    prev = jnp.where(i > 0, pprev_ref[0], 0.0)
    nxt = jnp.where(i < n_tiles - 1, pnext_ref[0], 0.0)
    e = jnp.concatenate([prev, p, nxt], axis=0)
    n_ext = tm + 2 * POOL_HALO

    def ahead(a, k):
        return pltpu.roll(a, n_ext - k, axis=0)

    a2 = e + ahead(e, 1)
    a4 = a2 + ahead(a2, 2)
    a8 = a4 + ahead(a4, 4)
    a16 = a8 + ahead(a8, 8)
    s2 = ahead(a2, POOL_HALO - 1)[:tm]
    s4 = ahead(a4, POOL_HALO - 2)[:tm]
    s8 = ahead(a8, POOL_HALO - 4)[:tm]
    s16 = a16[:tm]
    grp = lax.broadcasted_iota(jnp.int32, (tm, POOL_W), 1) >> 6
    win_sum = jnp.where(grp == 0, s2, jnp.where(grp == 1, s4, jnp.where(grp == 2, s8, s16)))
    t = i * tm + lax.broadcasted_iota(jnp.int32, (tm, POOL_W), 0)
    half = jnp.left_shift(1, grp)
    cnt = jnp.minimum(t + half, seq_len) - jnp.maximum(t - half, 0)
    d = win_sum / cnt.astype(F32) - p
    pool = jnp.dot(d.astype(BF16), pw_ref[...], preferred_element_type=F32) * ps_ref[...]

    cat = jnp.concatenate([attn_ref[0], sgu.astype(BF16), pool.astype(BF16)], axis=-1)
    mix = jnp.dot(cat, wo_ref[...], preferred_element_type=F32)
    x1 = x_ref[0] + m[2:3] * mix
    h2 = (_rms(x1, n2_ref[...]) * (1.0 + m[4:5]) + m[3:4]).astype(BF16)
    return x1, h2


def _ffn_stage(x1, h2, gate2, wg_ref, wu_ref, wd_ref, fn_ref, final):
    gate = jnp.dot(h2, wg_ref[...], preferred_element_type=F32)
    up = jnp.dot(h2, wu_ref[...], preferred_element_type=F32)
    act = (_silu(gate) * up).astype(BF16)
    x2 = x1 + gate2 * jnp.dot(act, wd_ref[...], preferred_element_type=F32)
    if final:
        x2 = _rms(x2, fn_ref[...])
    return x2


def _mixffn_kernel(x_ref, mod_ref, modp_ref, attn_ref, u_ref, vn_ref, p_ref, pprev_ref, pnext_ref,
                   ws_ref, bs_ref, pw_ref, ps_ref, wo_ref, n2_ref, wg_ref, wu_ref, wd_ref,
                   fn_ref, o_ref, x1_scr, h2_scr, *, seq_len, tiles_per_seq, final):
    s = pl.program_id(0)
    n_steps = pl.num_programs(0)
    i = lax.rem(jnp.minimum(s, n_steps - 2), tiles_per_seq)
    mix_args = (i, tiles_per_seq, x_ref, mod_ref, attn_ref, u_ref, vn_ref, p_ref, pprev_ref,
                pnext_ref, ws_ref, bs_ref, pw_ref, ps_ref, wo_ref, n2_ref, seq_len)
    ffn_args = (wg_ref, wu_ref, wd_ref, fn_ref, final)

    @pl.when(s == 0)
    def _():
        x1, h2 = _mix_stage(*mix_args)
        x1_scr[...] = x1
        h2_scr[...] = h2

    @pl.when(jnp.logical_and(s > 0, s < n_steps - 1))
    def _():
        x1_prev = x1_scr[...]
        h2_prev = h2_scr[...]
        o_ref[0] = _ffn_stage(x1_prev, h2_prev, modp_ref[0][5:6], *ffn_args)
        x1, h2 = _mix_stage(*mix_args)
        x1_scr[...] = x1
        h2_scr[...] = h2

    @pl.when(s == n_steps - 1)
    def _():
        o_ref[0] = _ffn_stage(x1_scr[...], h2_scr[...], modp_ref[0][5:6], *ffn_args)


def _mixffn(layer, x, mod, mod_row_of_batch, attn, u, vn, p, ws, bs, pw, ps, wo, n2, wg, wu, wd,
            fn, *, final):
    B, N, _ = x.shape
    tm = min(TOKEN_TILE, N)
    tps = N // tm
    n_tiles = B * tps
    hb = tm // POOL_HALO
    n_hblocks = N // POOL_HALO

    def cur(s):
        t = jnp.minimum(s, n_tiles - 1)
        return t // tps, lax.rem(t, tps)

    def prv(s):
        t = jnp.maximum(s - 1, 0)
        return t // tps, lax.rem(t, tps)

    def tile(w):
        return pl.BlockSpec((1, tm, w), lambda s: (*cur(s), 0))

    def halo_prev(s):
        b, i = cur(s)
        return b, jnp.maximum(i * hb - 1, 0), 0

    def halo_next(s):
        b, i = cur(s)
        return b, jnp.minimum((i + 1) * hb, n_hblocks - 1), 0

    once = dict(pipeline_mode=pl.Buffered(1))
    return pl.pallas_call(
        functools.partial(_mixffn_kernel, seq_len=N, tiles_per_seq=tps, final=final),
        grid=(n_tiles + 1,),
        in_specs=[
            tile(D_MODEL),
            pl.BlockSpec((None, 1, 6, D_MODEL),
                         lambda s: (layer, mod_row_of_batch(cur(s)[0]), 0, 0)),
            pl.BlockSpec((None, 1, 6, D_MODEL),
                         lambda s: (layer, mod_row_of_batch(prv(s)[0]), 0, 0)),
            tile(ATTN_W), tile(SGU_W), tile(SGU_W), tile(POOL_W),
            pl.BlockSpec((1, POOL_HALO, POOL_W), halo_prev),
            pl.BlockSpec((1, POOL_HALO, POOL_W), halo_next),
            _layer_spec(layer, SGU_GROUPS, CHUNK, CHUNK, **once),
            _layer_spec(layer, CHUNK, SGU_W, **once),
            _layer_spec(layer, POOL_W, POOL_W, **once),
            _layer_spec(layer, 1, POOL_W, **once),
            _layer_spec(layer, MIX_W, D_MODEL, **once),
            _layer_spec(layer, 1, D_MODEL, **once),
            _layer_spec(layer, D_MODEL, D_FF, **once),
            _layer_spec(layer, D_MODEL, D_FF, **once),
            _layer_spec(layer, D_FF, D_MODEL, **once),
            pl.BlockSpec((1, D_MODEL), lambda s: (0, 0), **once),
        ],
        out_specs=pl.BlockSpec((1, tm, D_MODEL), lambda s: (*prv(s), 0)),
        out_shape=jax.ShapeDtypeStruct((B, N, D_MODEL), F32),
        scratch_shapes=[pltpu.VMEM((tm, D_MODEL), F32), pltpu.VMEM((tm, D_MODEL), BF16)],
        compiler_params=pltpu.CompilerParams(
            dimension_semantics=("arbitrary",), vmem_limit_bytes=VMEM_LIMIT),
        name="mixffn_final" if final else "mixffn",
    )(x, mod, mod, attn, u, vn, p, p, p, ws, bs, pw, ps, wo, n2, wg, wu, wd, fn)


def _rope_tables(n_tokens):
    rows = jnp.repeat(jnp.arange(n_tokens // GRID_W, dtype=F32), GRID_W)
    cols = jnp.tile(jnp.arange(GRID_W, dtype=F32), n_tokens // GRID_W)
    inv = ROPE_THETA ** (-jnp.arange(ROPE_PAIRS_PER_AXIS, dtype=F32) / ROPE_PAIRS_PER_AXIS)
    ang = jnp.concatenate([rows[:, None] * inv, cols[:, None] * inv], axis=-1)
    cos, sin = jnp.cos(ang), jnp.sin(ang)
    return jnp.tile(cos, (1, 4)), jnp.tile(jnp.concatenate([-sin, sin], axis=-1), (1, 2))


def _block_diag(w):
    g, a, b = w.shape
    out = jnp.zeros((g * a, g * b), w.dtype)
    for k in range(g):
        out = out.at[k * a:(k + 1) * a, k * b:(k + 1) * b].set(w[k])
    return out


def kernel(x, c, ctx, c_ctx, w_mod, b_mod, norm1, norm2, w_in, q_norm, k_norm, sgu_norm, w_s, b_s,
           pool_w, pool_scale, w_out, w_gate, w_up, w_down, final_norm):
    B, S, _ = x.shape
    L = ctx.shape[1]
    c_rows = jnp.concatenate(
        [c, c_ctx[None, :], jnp.zeros((MOD_ROWS - B - 1, D_MODEL), F32)], axis=0)
    mod = _modulation(c_rows, w_mod, b_mod).reshape(DEPTH, MOD_ROWS, 6, D_MODEL)
    cos, sin = _rope_tables(S)
    ones = jnp.ones((L, LANES), F32)

    head_order = jnp.array([4 * hh + s for s in range(ATTN_W // LANES) for hh in range(N_KV_HEADS)])
    q_cols = (head_order[:, None] * HEAD_DIM + jnp.arange(HEAD_DIM)[None, :]).reshape(-1)
    in_cols = jnp.concatenate([q_cols, jnp.arange(ATTN_W, IN_W)])
    out_rows = jnp.concatenate([q_cols, jnp.arange(ATTN_W, MIX_W)])

    row = lambda a: a[:, None, :]
    proj_w = (
        row(norm1),
        w_in[:, :, in_cols].astype(BF16),
        row(jnp.tile(q_norm, (1, 2))),
        row(jnp.tile(k_norm, (1, 2))),
        row(sgu_norm),
    )
    mix_w = (
        w_s.astype(BF16),
        jnp.repeat(jnp.swapaxes(b_s, 1, 2), SGU_W // SGU_GROUPS, axis=2),
        jnp.stack([_block_diag(pool_w[l]) for l in range(DEPTH)]).astype(BF16),
        row(pool_scale),
        w_out[:, out_rows, :].astype(BF16),
        row(norm2),
        w_gate.astype(BF16),
        w_up.astype(BF16),
        w_down.astype(BF16),
        final_norm[None, :],
    )
    lat_row = lambda b: b
    ctx_row = lambda b: B

    for l in range(DEPTH):
        last = l == DEPTH - 1
        q, k, vt, u, vn, p = _inproj(l, x, mod, lat_row, *proj_w, cos, sin, rope=True)
        qc, kc, vtc, uc, vnc, pc = _inproj(l, ctx, mod, ctx_row, *proj_w, ones, ones, rope=False)
        attn = _attention(q, [(kc, vtc), (k, vt)])
        if not last:
            attn_c = _attention(qc, [(kc, vtc)])
            ctx = _mixffn(l, ctx, mod, ctx_row, attn_c, uc, vnc, pc, *mix_w, final=False)
        x = _mixffn(l, x, mod, lat_row, attn, u, vn, p, *mix_w, final=last)
    return x
```

```python
import functools
import math

import jax
import jax.numpy as jnp
from jax import lax
from jax.experimental import pallas as pl
from jax.experimental.pallas import tpu as pltpu

F32 = jnp.float32
BF16 = jnp.bfloat16

D_MODEL = 1024
DEPTH = 2
GRID_W = 64
EPS = 1e-6

HEAD_DIM = 64
N_HEADS = 8
N_KV_HEADS = 2
Q_PER_KV = N_HEADS // N_KV_HEADS
ATTN_W = N_HEADS * HEAD_DIM
KV_W = N_KV_HEADS * HEAD_DIM
ROPE_THETA = 10000.0
ROPE_PAIRS_PER_AXIS = HEAD_DIM // 4
ATTN_SCALE = HEAD_DIM ** -0.5
LOG2E = math.log2(math.e)

CHUNK = 128
SGU_W = D_MODEL // 4
SGU_GROUPS = 4
POOL_WINDOWS = (2, 4, 8, 16)
POOL_W = D_MODEL // 4
POOL_HALO = 8

MIX_W = ATTN_W + SGU_W + POOL_W
O_Q = 0
O_K = O_Q + ATTN_W
O_V = O_K + KV_W
O_G = O_V + KV_W
O_P = O_G + 2 * SGU_W
IN_W = O_P + POOL_W
D_FF = 2816

LANES = 128
MXU_COLS = 256
MOD_ROWS = 16
VMEM_LIMIT = 56 * 1024 * 1024

TOKEN_TILE = 256
INPROJ_TILE = 1024
SUB_TILE = 256
Q_TILE = 512


def _half_mask(shape):
    lane = lax.broadcasted_iota(jnp.int32, shape, len(shape) - 1)
    return (lane & HEAD_DIM) == 0


def _rms(x, gain):
    return x * lax.rsqrt(jnp.mean(x * x, axis=-1, keepdims=True) + EPS) * gain


def _head_rms_slab(z, gain):
    lo = _half_mask(z.shape)
    sq = z * z
    s0 = jnp.sum(jnp.where(lo, sq, 0.0), axis=-1, keepdims=True)
    s1 = jnp.sum(jnp.where(lo, 0.0, sq), axis=-1, keepdims=True)
    ms = jnp.where(lo, s0, s1) * (1.0 / HEAD_DIM)
    return z * lax.rsqrt(ms + EPS) * gain


def _rope_slab(x, cos, sin_signed):
    lane = lax.broadcasted_iota(jnp.int32, x.shape, 1)
    first = (lane & (HEAD_DIM // 2)) == 0
    up = pltpu.roll(x, LANES - HEAD_DIM // 2, axis=1)
    down = pltpu.roll(x, HEAD_DIM // 2, axis=1)
    partner = jnp.where(first, up, down)
    return x * cos + partner * sin_signed


def _gelu_tanh(x):
    c = math.sqrt(2.0 / math.pi)
    return 0.5 * x * (1.0 + jnp.tanh(c * (x + 0.044715 * (x * x * x))))


def _silu(x):
    return x / (1.0 + jnp.exp(-x))


def _mod_kernel(c_ref, w_ref, b_ref, o_ref):
    a = _silu(c_ref[...])
    w = w_ref[0]
    a_hi = a.astype(BF16)
    a_lo = (a - a_hi.astype(F32)).astype(BF16)
    w_hi = w.astype(BF16)
    w_lo = (w - w_hi.astype(F32)).astype(BF16)
    dot = functools.partial(jnp.dot, preferred_element_type=F32)
    o_ref[0] = dot(a_hi, w_hi) + (dot(a_lo, w_hi) + dot(a_hi, w_lo)) + b_ref[0]


def _modulation(c_rows, w_mod, b_mod):
    n_col = 6
    return pl.pallas_call(
        _mod_kernel,
        grid=(DEPTH, n_col),
        in_specs=[
            pl.BlockSpec((MOD_ROWS, D_MODEL), lambda l, j: (0, 0)),
            pl.BlockSpec((1, D_MODEL, D_MODEL), lambda l, j: (l, 0, j)),
            pl.BlockSpec((1, 1, D_MODEL), lambda l, j: (l, 0, j)),
        ],
        out_specs=pl.BlockSpec((1, MOD_ROWS, D_MODEL), lambda l, j: (l, 0, j)),
        out_shape=jax.ShapeDtypeStruct((DEPTH, MOD_ROWS, 6 * D_MODEL), F32),
        compiler_params=pltpu.CompilerParams(dimension_semantics=("parallel", "parallel")),
        name="modulation",
    )(c_rows, w_mod, b_mod.reshape(DEPTH, 1, 6 * D_MODEL))


def _inproj_kernel(x_ref, mod_ref, n1_ref, w_ref, qg_ref, kg_ref, sg_ref, cos_ref, sin_ref,
                   q_ref, k_ref, vt_ref, u_ref, vn_ref, p_ref, *, rope):
    m = mod_ref[0]
    tm = x_ref.shape[1]
    sub = min(SUB_TILE, tm)

    def project(r0):
        rows = slice(r0, r0 + sub)
        h = _rms(x_ref[0, rows, :], n1_ref[...]) * (1.0 + m[1:2]) + m[0:1]
        return jnp.dot(h.astype(BF16), w_ref[...], preferred_element_type=F32)

    def finish(r0, z):
        rows = slice(r0, r0 + sub)
        if rope:
            cos = cos_ref[rows, :]
            sin = sin_ref[rows, :]

        for j in range(ATTN_W // LANES):
            zq = z[:, O_Q + j * LANES:O_Q + (j + 1) * LANES]
            qn = _head_rms_slab(zq, qg_ref[...])
            if rope:
                qn = _rope_slab(qn, cos, sin)
            q_ref[0, rows, j * LANES:(j + 1) * LANES] = (qn * (ATTN_SCALE * LOG2E)).astype(BF16)

        kn = _head_rms_slab(z[:, O_K:O_K + KV_W], kg_ref[...])
        if rope:
            kn = _rope_slab(kn, cos, sin)
        k_swapped = pltpu.roll(kn, HEAD_DIM, axis=1)
        lo = _half_mask(kn.shape)
        k_ref[0, 0, rows, :] = jnp.where(lo, kn, k_swapped).astype(BF16)
        k_ref[0, 1, rows, :] = jnp.where(lo, k_swapped, kn).astype(BF16)

        vt = z[:, O_V:O_V + KV_W].T
        ones = jnp.ones((HEAD_DIM, sub), F32)
        for hh in range(N_KV_HEADS):
            vt_ref[0, hh, :, rows] = jnp.concatenate(
                [vt[hh * HEAD_DIM:(hh + 1) * HEAD_DIM], ones], axis=0).astype(BF16)

        g = _gelu_tanh(z[:, O_G:O_P])
        u_ref[0, rows, :] = g[:, :SGU_W]
        vn_ref[0, rows, :] = _rms(g[:, SGU_W:], sg_ref[...]).astype(BF16)

        p_ref[0, rows, :] = z[:, O_P:]

    starts = list(range(0, tm, sub))
    z = project(starts[0])
    for r0, r1 in zip(starts, starts[1:] + [None]):
        z_next = project(r1) if r1 is not None else None
        finish(r0, z)
        z = z_next


def _layer_spec(layer, *shape, **kwargs):
    return pl.BlockSpec((None, *shape), lambda *_: (layer,) + (0,) * len(shape), **kwargs)


def _inproj(layer, x, mod, mod_row_of_batch, norm1, w_in, qg, kg, sg, cos, sin, *, rope):
    B, N, _ = x.shape
    tm = min(INPROJ_TILE, N)
    grid = (B, N // tm)
    return pl.pallas_call(
        functools.partial(_inproj_kernel, rope=rope),
        grid=grid,
        in_specs=[
            pl.BlockSpec((1, tm, D_MODEL), lambda b, i: (b, i, 0)),
            pl.BlockSpec((None, 1, 6, D_MODEL), lambda b, i: (layer, mod_row_of_batch(b), 0, 0)),
            _layer_spec(layer, 1, D_MODEL),
            _layer_spec(layer, D_MODEL, IN_W),
            _layer_spec(layer, 1, LANES),
            _layer_spec(layer, 1, LANES),
            _layer_spec(layer, 1, SGU_W),
            pl.BlockSpec((tm, LANES), lambda b, i: (i, 0)),
            pl.BlockSpec((tm, LANES), lambda b, i: (i, 0)),
        ],
        out_specs=[
            pl.BlockSpec((1, tm, ATTN_W), lambda b, i: (b, i, 0)),
            pl.BlockSpec((1, N_KV_HEADS, tm, KV_W), lambda b, i: (b, 0, i, 0)),
            pl.BlockSpec((1, N_KV_HEADS, 2 * HEAD_DIM, tm), lambda b, i: (b, 0, 0, i)),
            pl.BlockSpec((1, tm, SGU_W), lambda b, i: (b, i, 0)),
            pl.BlockSpec((1, tm, SGU_W), lambda b, i: (b, i, 0)),
            pl.BlockSpec((1, tm, POOL_W), lambda b, i: (b, i, 0)),
        ],
        out_shape=[
            jax.ShapeDtypeStruct((B, N, ATTN_W), BF16),
            jax.ShapeDtypeStruct((B, N_KV_HEADS, N, KV_W), BF16),
            jax.ShapeDtypeStruct((B, N_KV_HEADS, 2 * HEAD_DIM, N), BF16),
            jax.ShapeDtypeStruct((B, N, SGU_W), F32),
            jax.ShapeDtypeStruct((B, N, SGU_W), BF16),
            jax.ShapeDtypeStruct((B, N, POOL_W), F32),
        ],
        compiler_params=pltpu.CompilerParams(
            dimension_semantics=("parallel", "parallel"), vmem_limit_bytes=VMEM_LIMIT),
        name="inproj_rope" if rope else "inproj",
    )(x, mod, norm1, w_in, qg, kg, sg, cos, sin)


def _attn_kernel(*refs, n_kv_sets):
    q_ref = refs[0]
    k_refs = refs[1:1 + n_kv_sets]
    vt_refs = refs[1 + n_kv_sets:1 + 2 * n_kv_sets]
    o_ref = refs[1 + 2 * n_kv_sets]
    tq = q_ref.shape[1]
    lo = _half_mask((tq, LANES))
    zero = jnp.zeros((tq, LANES), BF16)
    nt = (((1,), (1,)), ((), ()))

    def scores(idx):
        slab, half, kvh = idx // 2, idx % 2, idx // Q_PER_KV
        qs = q_ref[0, :, slab * LANES:(slab + 1) * LANES]
        qm = jnp.where(lo, qs, zero) if half == 0 else jnp.where(lo, zero, qs)
        st = [lax.dot_general(k[0, kvh], qm, nt, preferred_element_type=F32) for k in k_refs]
        mx = functools.reduce(jnp.maximum, [jnp.max(si, axis=0, keepdims=True) for si in st])
        return st, mx

    def weighted_values(idx, st, mx):
        kvh = idx // Q_PER_KV
        acc = functools.reduce(lambda a, b: a + b, [
            jnp.dot(vt[0, kvh], jnp.exp2(si - mx).astype(BF16), preferred_element_type=F32)
            for si, vt in zip(st, vt_refs)])
        return acc[:HEAD_DIM] / acc[HEAD_DIM:HEAD_DIM + 1]

    heads = []
    pending = scores(0)
    for idx in range(N_HEADS):
        nxt = scores(idx + 1) if idx + 1 < N_HEADS else None
        heads.append(weighted_values(idx, *pending))
        pending = nxt
    o_ref[0] = jnp.concatenate(heads, axis=0).T.astype(BF16)


def _attention(q, kv_sets):
    B, N, _ = q.shape
    tq = min(Q_TILE, N)
    in_specs = [pl.BlockSpec((1, tq, ATTN_W), lambda b, i: (b, i, 0))]
    for k, _ in kv_sets:
        in_specs.append(pl.BlockSpec((1, N_KV_HEADS, k.shape[2], KV_W), lambda b, i: (b, 0, 0, 0)))
    for _, vt in kv_sets:
        in_specs.append(pl.BlockSpec((1, N_KV_HEADS, 2 * HEAD_DIM, vt.shape[3]),
                                     lambda b, i: (b, 0, 0, 0)))
    return pl.pallas_call(
        functools.partial(_attn_kernel, n_kv_sets=len(kv_sets)),
        grid=(B, N // tq),
        in_specs=in_specs,
        out_specs=pl.BlockSpec((1, tq, ATTN_W), lambda b, i: (b, i, 0)),
        out_shape=jax.ShapeDtypeStruct((B, N, ATTN_W), BF16),
        compiler_params=pltpu.CompilerParams(
            dimension_semantics=("parallel", "parallel"), vmem_limit_bytes=VMEM_LIMIT),
        name=f"attention_{len(kv_sets)}",
    )(q, *[k for k, _ in kv_sets], *[vt for _, vt in kv_sets])


def _mix_stage(i, n_tiles, x_ref, mod_ref, attn_ref, u_ref, vn_ref, p_ref, pprev_ref, pnext_ref,
               ws_ref, bs_ref, pw_ref, ps_ref, wo_ref, n2_ref, seq_len):
    tm = x_ref.shape[1]
    m = mod_ref[0]

    lo = _half_mask((CHUNK, LANES))
    sgu_chunks = []
    for c in range(tm // CHUNK):
        slabs = []
        for j in range(SGU_W // LANES):
            vs = vn_ref[0, c * CHUNK:(c + 1) * CHUNK, j * LANES:(j + 1) * LANES]
            ra = jnp.dot(ws_ref[2 * j], vs, preferred_element_type=F32)
            rb = jnp.dot(ws_ref[2 * j + 1], vs, preferred_element_type=F32)
            slabs.append(jnp.where(lo, ra, rb))
        mixed = jnp.concatenate(slabs, axis=-1) + bs_ref[...]
        sgu_chunks.append((u_ref[0, c * CHUNK:(c + 1) * CHUNK, :] * mixed).astype(BF16))
        yield
    sgu = jnp.concatenate(sgu_chunks, axis=0)

    p = p_ref[0]
    prev = jnp.where(i > 0, pprev_ref[0], 0.0)
    nxt = jnp.where(i < n_tiles - 1, pnext_ref[0], 0.0)
    e = jnp.concatenate([prev, p, nxt], axis=0)
    n_ext = tm + 2 * POOL_HALO

    def ahead(a, k):
        return pltpu.roll(a, n_ext - k, axis=0)

    a2 = e + ahead(e, 1)
    a4 = a2 + ahead(a2, 2)
    a8 = a4 + ahead(a4, 4)
    a16 = a8 + ahead(a8, 8)
    s2 = ahead(a2, POOL_HALO - 1)[:tm]
    s4 = ahead(a4, POOL_HALO - 2)[:tm]
    s8 = ahead(a8, POOL_HALO - 4)[:tm]
    s16 = a16[:tm]
    yield
    grp = lax.broadcasted_iota(jnp.int32, (tm, POOL_W), 1) >> 6
    win_sum = jnp.where(grp == 0, s2, jnp.where(grp == 1, s4, jnp.where(grp == 2, s8, s16)))
    t = i * tm + lax.broadcasted_iota(jnp.int32, (tm, POOL_W), 0)
    half = jnp.left_shift(1, grp)
    cnt = jnp.minimum(t + half, seq_len) - jnp.maximum(t - half, 0)
    d = win_sum / cnt.astype(F32) - p
    pool = jnp.dot(d.astype(BF16), pw_ref[...], preferred_element_type=F32) * ps_ref[...]

    yield

    cat = jnp.concatenate([attn_ref[0], sgu, pool.astype(BF16)], axis=-1)
    x1_cols = []
    for n in range(0, D_MODEL, MXU_COLS):
        cols = slice(n, n + MXU_COLS)
        mix = jnp.dot(cat, wo_ref[:, cols], preferred_element_type=F32)
        x1_cols.append(x_ref[0, :, cols] + m[2:3, cols] * mix)
        yield
    x1 = jnp.concatenate(x1_cols, axis=-1)
    h2 = (_rms(x1, n2_ref[...]) * (1.0 + m[4:5]) + m[3:4]).astype(BF16)
    return x1, h2


def _ffn_stage(x1, h2, gate2, wg_ref, wu_ref, wd_ref, fn_ref, final):
    acts = []
    for n in range(0, D_FF, MXU_COLS):
        cols = slice(n, n + MXU_COLS)
        gate = jnp.dot(h2, wg_ref[:, cols], preferred_element_type=F32)
        up = jnp.dot(h2, wu_ref[:, cols], preferred_element_type=F32)
        acts.append((_silu(gate) * up).astype(BF16))
        yield
    act = jnp.concatenate(acts, axis=-1)
    x2_cols = []
    for n in range(0, D_MODEL, MXU_COLS):
        cols = slice(n, n + MXU_COLS)
        down = jnp.dot(act, wd_ref[:, cols], preferred_element_type=F32)
        x2_cols.append(x1[:, cols] + gate2[:, cols] * down)
        yield
    x2 = jnp.concatenate(x2_cols, axis=-1)
    if final:
        x2 = _rms(x2, fn_ref[...])
    return x2


def _drive(*gens):
    results = [None] * len(gens)
    live = list(range(len(gens)))
    while live:
        for g in list(live):
            try:
                next(gens[g])
            except StopIteration as stop:
                results[g] = stop.value
                live.remove(g)
    return results


def _mixffn_kernel(x_ref, mod_ref, modp_ref, attn_ref, u_ref, vn_ref, p_ref, pprev_ref, pnext_ref,
                   ws_ref, bs_ref, pw_ref, ps_ref, wo_ref, n2_ref, wg_ref, wu_ref, wd_ref,
                   fn_ref, o_ref, x1_scr, h2_scr, *, seq_len, tiles_per_seq, final):
    s = pl.program_id(0)
    n_steps = pl.num_programs(0)
    i = lax.rem(jnp.minimum(s, n_steps - 2), tiles_per_seq)
    mix_args = (i, tiles_per_seq, x_ref, mod_ref, attn_ref, u_ref, vn_ref, p_ref, pprev_ref,
                pnext_ref, ws_ref, bs_ref, pw_ref, ps_ref, wo_ref, n2_ref, seq_len)
    ffn_args = (wg_ref, wu_ref, wd_ref, fn_ref, final)

    @pl.when(s == 0)
    def _():
        (x1, h2), = _drive(_mix_stage(*mix_args))
        x1_scr[...] = x1
        h2_scr[...] = h2

    @pl.when(jnp.logical_and(s > 0, s < n_steps - 1))
    def _():
        ffn = _ffn_stage(x1_scr[...], h2_scr[...], modp_ref[0][5:6], *ffn_args)
        x2, (x1, h2) = _drive(ffn, _mix_stage(*mix_args))
        o_ref[0] = x2
        x1_scr[...] = x1
        h2_scr[...] = h2

    @pl.when(s == n_steps - 1)
    def _():
        o_ref[0], = _drive(_ffn_stage(x1_scr[...], h2_scr[...], modp_ref[0][5:6], *ffn_args))


def _mixffn(layer, x, mod, mod_row_of_batch, attn, u, vn, p, ws, bs, pw, ps, wo, n2, wg, wu, wd,
            fn, *, final):
    B, N, _ = x.shape
    tm = min(TOKEN_TILE, N)
    tps = N // tm
    n_tiles = B * tps
    hb = tm // POOL_HALO
    n_hblocks = N // POOL_HALO

    def cur(s):
        t = jnp.minimum(s, n_tiles - 1)
        return t // tps, lax.rem(t, tps)

    def prv(s):
        t = jnp.maximum(s - 1, 0)
        return t // tps, lax.rem(t, tps)

    def tile(w):
        return pl.BlockSpec((1, tm, w), lambda s: (*cur(s), 0))

    def halo_prev(s):
        b, i = cur(s)
        return b, jnp.maximum(i * hb - 1, 0), 0

    def halo_next(s):
        b, i = cur(s)
        return b, jnp.minimum((i + 1) * hb, n_hblocks - 1), 0

    once = dict(pipeline_mode=pl.Buffered(1))
    return pl.pallas_call(
        functools.partial(_mixffn_kernel, seq_len=N, tiles_per_seq=tps, final=final),
        grid=(n_tiles + 1,),
        in_specs=[
            tile(D_MODEL),
            pl.BlockSpec((None, 1, 6, D_MODEL),
                         lambda s: (layer, mod_row_of_batch(cur(s)[0]), 0, 0)),
            pl.BlockSpec((None, 1, 6, D_MODEL),
                         lambda s: (layer, mod_row_of_batch(prv(s)[0]), 0, 0)),
            tile(ATTN_W), tile(SGU_W), tile(SGU_W), tile(POOL_W),
            pl.BlockSpec((1, POOL_HALO, POOL_W), halo_prev),
            pl.BlockSpec((1, POOL_HALO, POOL_W), halo_next),
            _layer_spec(layer, SGU_GROUPS, CHUNK, CHUNK, **once),
            _layer_spec(layer, CHUNK, SGU_W, **once),
            _layer_spec(layer, POOL_W, POOL_W, **once),
            _layer_spec(layer, 1, POOL_W, **once),
            _layer_spec(layer, MIX_W, D_MODEL, **once),
            _layer_spec(layer, 1, D_MODEL, **once),
            _layer_spec(layer, D_MODEL, D_FF, **once),
            _layer_spec(layer, D_MODEL, D_FF, **once),
            _layer_spec(layer, D_FF, D_MODEL, **once),
            pl.BlockSpec((1, D_MODEL), lambda s: (0, 0), **once),
        ],
        out_specs=pl.BlockSpec((1, tm, D_MODEL), lambda s: (*prv(s), 0)),
        out_shape=jax.ShapeDtypeStruct((B, N, D_MODEL), F32),
        scratch_shapes=[pltpu.VMEM((tm, D_MODEL), F32), pltpu.VMEM((tm, D_MODEL), BF16)],
        compiler_params=pltpu.CompilerParams(
            dimension_semantics=("arbitrary",), vmem_limit_bytes=VMEM_LIMIT),
        name="mixffn_final" if final else "mixffn",
    )(x, mod, mod, attn, u, vn, p, p, p, ws, bs, pw, ps, wo, n2, wg, wu, wd, fn)


def _rope_tables(n_tokens):
    rows = jnp.repeat(jnp.arange(n_tokens // GRID_W, dtype=F32), GRID_W)
    cols = jnp.tile(jnp.arange(GRID_W, dtype=F32), n_tokens // GRID_W)
    inv = ROPE_THETA ** (-jnp.arange(ROPE_PAIRS_PER_AXIS, dtype=F32) / ROPE_PAIRS_PER_AXIS)
    ang = jnp.concatenate([rows[:, None] * inv, cols[:, None] * inv], axis=-1)
    cos, sin = jnp.cos(ang), jnp.sin(ang)
    return jnp.tile(cos, (1, 4)), jnp.tile(jnp.concatenate([-sin, sin], axis=-1), (1, 2))


def _block_diag(w):
    d, g, a, b = w.shape
    eye = jnp.eye(g, dtype=w.dtype)
    return (w[:, :, :, None, :] * eye[None, :, None, :, None]).reshape(d, g * a, g * b)


def kernel(x, c, ctx, c_ctx, w_mod, b_mod, norm1, norm2, w_in, q_norm, k_norm, sgu_norm, w_s, b_s,
           pool_w, pool_scale, w_out, w_gate, w_up, w_down, final_norm):
    B, S, _ = x.shape
    L = ctx.shape[1]
    c_rows = jnp.concatenate(
        [c, c_ctx[None, :], jnp.zeros((MOD_ROWS - B - 1, D_MODEL), F32)], axis=0)
    mod = _modulation(c_rows, w_mod, b_mod).reshape(DEPTH, MOD_ROWS, 6, D_MODEL)
    cos, sin = _rope_tables(S)
    ones = jnp.ones((L, LANES), F32)

    row = lambda a: a[:, None, :]
    proj_w = (
        row(norm1),
        w_in.astype(BF16),
        row(jnp.tile(q_norm, (1, 2))),
        row(jnp.tile(k_norm, (1, 2))),
        row(sgu_norm),
    )
    mix_w = (
        w_s.astype(BF16),
        jnp.repeat(jnp.swapaxes(b_s, 1, 2), SGU_W // SGU_GROUPS, axis=2),
        _block_diag(pool_w).astype(BF16),
        row(pool_scale),
        w_out.astype(BF16),
        row(norm2),
        w_gate.astype(BF16),
        w_up.astype(BF16),
        w_down.astype(BF16),
        final_norm[None, :],
    )
    lat_row = lambda b: b
    ctx_row = lambda b: B

    for l in range(DEPTH):
        last = l == DEPTH - 1
        q, k, vt, u, vn, p = _inproj(l, x, mod, lat_row, *proj_w, cos, sin, rope=True)
        qc, kc, vtc, uc, vnc, pc = _inproj(l, ctx, mod, ctx_row, *proj_w, ones, ones, rope=False)
        attn = _attention(q, [(kc, vtc), (k, vt)])
        if not last:
            attn_c = _attention(qc, [(kc, vtc)])
            ctx = _mixffn(l, ctx, mod, ctx_row, attn_c, uc, vnc, pc, *mix_w, final=False)
        x = _mixffn(l, x, mod, lat_row, attn, u, vn, p, *mix_w, final=last)
    return x
```

```python
import functools
import math

import jax
import jax.numpy as jnp
from jax import lax
from jax.experimental import pallas as pl
from jax.experimental.pallas import tpu as pltpu

F32 = jnp.float32
BF16 = jnp.bfloat16

D_MODEL = 1024
DEPTH = 2
GRID_W = 64
EPS = 1e-6

HEAD_DIM = 64
N_HEADS = 8
N_KV_HEADS = 2
Q_PER_KV = N_HEADS // N_KV_HEADS
ATTN_W = N_HEADS * HEAD_DIM
KV_W = N_KV_HEADS * HEAD_DIM
ROPE_THETA = 10000.0
ROPE_PAIRS_PER_AXIS = HEAD_DIM // 4
ATTN_SCALE = HEAD_DIM ** -0.5
LOG2E = math.log2(math.e)

CHUNK = 128
SGU_W = D_MODEL // 4
SGU_GROUPS = 4
POOL_WINDOWS = (2, 4, 8, 16)
POOL_W = D_MODEL // 4
POOL_HALO = 8

MIX_W = ATTN_W + SGU_W + POOL_W
O_Q = 0
O_K = O_Q + ATTN_W
O_V = O_K + KV_W
O_G = O_V + KV_W
O_P = O_G + 2 * SGU_W
IN_W = O_P + POOL_W
D_FF = 2816

LANES = 128
BF16_ROWS = 16
MOD_ROWS = 16
VMEM_LIMIT = 56 * 1024 * 1024

TOKEN_TILE = 256
INPROJ_TILE = 1024
SUB_TILE = 256
Q_TILE = 512


def _half_mask(shape):
    lane = lax.broadcasted_iota(jnp.int32, shape, len(shape) - 1)
    return (lane & HEAD_DIM) == 0


def _rms(x, gain):
    return x * lax.rsqrt(jnp.mean(x * x, axis=-1, keepdims=True) + EPS) * gain


def _head_rms_slab(z, gain):
    lo = _half_mask(z.shape)
    sq = z * z
    s0 = jnp.sum(jnp.where(lo, sq, 0.0), axis=-1, keepdims=True)
    s1 = jnp.sum(jnp.where(lo, 0.0, sq), axis=-1, keepdims=True)
    ms = jnp.where(lo, s0, s1) * (1.0 / HEAD_DIM)
    return z * lax.rsqrt(ms + EPS) * gain


def _rope_slab(x, cos, sin_signed):
    lane = lax.broadcasted_iota(jnp.int32, x.shape, 1)
    first = (lane & (HEAD_DIM // 2)) == 0
    up = pltpu.roll(x, LANES - HEAD_DIM // 2, axis=1)
    down = pltpu.roll(x, HEAD_DIM // 2, axis=1)
    partner = jnp.where(first, up, down)
    return x * cos + partner * sin_signed


def _gelu_tanh(x):
    c = math.sqrt(2.0 / math.pi)
    return 0.5 * x * (1.0 + jnp.tanh(c * (x + 0.044715 * (x * x * x))))


def _silu(x):
    return x / (1.0 + jnp.exp(-x))


def _mod_kernel(c_ref, w_ref, b_ref, win_ref, o_ref, win_bf_ref):
    win_bf_ref[...] = win_ref[...].astype(BF16)

    a = _silu(c_ref[...])
    w = w_ref[0]
    a_hi = a.astype(BF16)
    a_lo = (a - a_hi.astype(F32)).astype(BF16)
    w_hi = w.astype(BF16)
    w_lo = (w - w_hi.astype(F32)).astype(BF16)
    dot = functools.partial(jnp.dot, preferred_element_type=F32)
    o_ref[0] = dot(a_hi, w_hi) + (dot(a_lo, w_hi) + dot(a_hi, w_lo)) + b_ref[0]


def _modulation(c_rows, w_mod, b_mod, w_in):
    n_col = 6
    win_cols = IN_W // n_col
    win_spec = pl.BlockSpec((None, D_MODEL, win_cols), lambda l, j: (l, 0, j))
    return pl.pallas_call(
        _mod_kernel,
        grid=(DEPTH, n_col),
        in_specs=[
            pl.BlockSpec((MOD_ROWS, D_MODEL), lambda l, j: (0, 0)),
            pl.BlockSpec((1, D_MODEL, D_MODEL), lambda l, j: (l, 0, j)),
            pl.BlockSpec((1, 1, D_MODEL), lambda l, j: (l, 0, j)),
            win_spec,
        ],
        out_specs=[pl.BlockSpec((1, MOD_ROWS, D_MODEL), lambda l, j: (l, 0, j)), win_spec],
        out_shape=[jax.ShapeDtypeStruct((DEPTH, MOD_ROWS, 6 * D_MODEL), F32),
                   jax.ShapeDtypeStruct(w_in.shape, BF16)],
        compiler_params=pltpu.CompilerParams(dimension_semantics=("parallel", "parallel")),
        name="modulation",
    )(c_rows, w_mod, b_mod.reshape(DEPTH, 1, 6 * D_MODEL), w_in)


def _inproj_kernel(x_ref, mod_ref, n1_ref, w_ref, qg_ref, kg_ref, sg_ref, cos_ref, sin_ref,
                   q_ref, k_ref, vt_ref, u_ref, vn_ref, p_ref, *, rope):
    m = mod_ref[0]
    tm = x_ref.shape[1]
    sub = min(SUB_TILE, tm)

    def project(r0):
        rows = slice(r0, r0 + sub)
        h = _rms(x_ref[0, rows, :], n1_ref[...]) * (1.0 + m[1:2]) + m[0:1]
        return jnp.dot(h.astype(BF16), w_ref[...], preferred_element_type=F32)

    def finish(r0, z):
        rows = slice(r0, r0 + sub)
        if rope:
            cos = cos_ref[rows, :]
            sin = sin_ref[rows, :]

        for j in range(ATTN_W // LANES):
            zq = z[:, O_Q + j * LANES:O_Q + (j + 1) * LANES]
            qn = _head_rms_slab(zq, qg_ref[...])
            if rope:
                qn = _rope_slab(qn, cos, sin)
            q_ref[0, rows, j * LANES:(j + 1) * LANES] = (qn * (ATTN_SCALE * LOG2E)).astype(BF16)

        kn = _head_rms_slab(z[:, O_K:O_K + KV_W], kg_ref[...])
        if rope:
            kn = _rope_slab(kn, cos, sin)
        k_swapped = pltpu.roll(kn, HEAD_DIM, axis=1)
        lo = _half_mask(kn.shape)
        k_ref[0, 0, rows, :] = jnp.where(lo, kn, k_swapped).astype(BF16)
        k_ref[0, 1, rows, :] = jnp.where(lo, k_swapped, kn).astype(BF16)

        vt = z[:, O_V:O_V + KV_W].T
        ones = jnp.ones((HEAD_DIM, sub), F32)
        for hh in range(N_KV_HEADS):
            vt_ref[0, hh, :, rows] = jnp.concatenate(
                [vt[hh * HEAD_DIM:(hh + 1) * HEAD_DIM], ones], axis=0).astype(BF16)

        g = _gelu_tanh(z[:, O_G:O_P])
        u_ref[0, rows, :] = g[:, :SGU_W]
        vn_ref[0, rows, :] = _rms(g[:, SGU_W:], sg_ref[...]).astype(BF16)

        p_ref[0, rows, :] = z[:, O_P:]

    starts = list(range(0, tm, sub))
    z = project(starts[0])
    for r0, r1 in zip(starts, starts[1:] + [None]):
        z_next = project(r1) if r1 is not None else None
        finish(r0, z)
        z = z_next


def _layer_spec(layer, *shape, **kwargs):
    return pl.BlockSpec((None, *shape), lambda *_: (layer,) + (0,) * len(shape), **kwargs)


def _inproj(layer, x, mod, mod_row_of_batch, norm1, w_in, qg, kg, sg, cos, sin, *, rope):
    B, N, _ = x.shape
    tm = min(INPROJ_TILE, N)
    grid = (B, N // tm)
    return pl.pallas_call(
        functools.partial(_inproj_kernel, rope=rope),
        grid=grid,
        in_specs=[
            pl.BlockSpec((1, tm, D_MODEL), lambda b, i: (b, i, 0)),
            pl.BlockSpec((None, 1, 6, D_MODEL), lambda b, i: (layer, mod_row_of_batch(b), 0, 0)),
            _layer_spec(layer, 1, D_MODEL),
            _layer_spec(layer, D_MODEL, IN_W),
            _layer_spec(layer, 1, LANES),
            _layer_spec(layer, 1, LANES),
            _layer_spec(layer, 1, SGU_W),
            pl.BlockSpec((tm, LANES), lambda b, i: (i, 0)),
            pl.BlockSpec((tm, LANES), lambda b, i: (i, 0)),
        ],
        out_specs=[
            pl.BlockSpec((1, tm, ATTN_W), lambda b, i: (b, i, 0)),
            pl.BlockSpec((1, N_KV_HEADS, tm, KV_W), lambda b, i: (b, 0, i, 0)),
            pl.BlockSpec((1, N_KV_HEADS, 2 * HEAD_DIM, tm), lambda b, i: (b, 0, 0, i)),
            pl.BlockSpec((1, tm, SGU_W), lambda b, i: (b, i, 0)),
            pl.BlockSpec((1, tm, SGU_W), lambda b, i: (b, i, 0)),
            pl.BlockSpec((1, tm, POOL_W), lambda b, i: (b, i, 0)),
        ],
        out_shape=[
            jax.ShapeDtypeStruct((B, N, ATTN_W), BF16),
            jax.ShapeDtypeStruct((B, N_KV_HEADS, N, KV_W), BF16),
            jax.ShapeDtypeStruct((B, N_KV_HEADS, 2 * HEAD_DIM, N), BF16),
            jax.ShapeDtypeStruct((B, N, SGU_W), F32),
            jax.ShapeDtypeStruct((B, N, SGU_W), BF16),
            jax.ShapeDtypeStruct((B, N, POOL_W), F32),
        ],
        compiler_params=pltpu.CompilerParams(
            dimension_semantics=("parallel", "parallel"), vmem_limit_bytes=VMEM_LIMIT),
        name="inproj_rope" if rope else "inproj",
    )(x, mod, norm1, w_in, qg, kg, sg, cos, sin)


def _attn_kernel(*refs, n_kv_sets, n_cast):
    q_ref = refs[0]
    k_refs = refs[1:1 + n_kv_sets]
    vt_refs = refs[1 + n_kv_sets:1 + 2 * n_kv_sets]
    n_in = 1 + 2 * n_kv_sets + n_cast
    o_ref = refs[n_in]
    for src, dst in zip(refs[n_in - n_cast:n_in], refs[n_in + 1:]):
        dst[...] = src[...].astype(BF16)
    tq = q_ref.shape[1]
    lo = _half_mask((tq, LANES))
    zero = jnp.zeros((tq, LANES), BF16)
    nt = (((1,), (1,)), ((), ()))

    def scores(idx):
        slab, half, kvh = idx // 2, idx % 2, idx // Q_PER_KV
        qs = q_ref[0, :, slab * LANES:(slab + 1) * LANES]
        qm = jnp.where(lo, qs, zero) if half == 0 else jnp.where(lo, zero, qs)
        st = [lax.dot_general(k[0, kvh], qm, nt, preferred_element_type=F32) for k in k_refs]
        mx = functools.reduce(jnp.maximum, [jnp.max(si, axis=0, keepdims=True) for si in st])
        return st, mx

    def weighted_values(idx, st, mx):
        kvh = idx // Q_PER_KV
        acc = functools.reduce(lambda a, b: a + b, [
            jnp.dot(vt[0, kvh], jnp.exp2(si - mx).astype(BF16), preferred_element_type=F32)
            for si, vt in zip(st, vt_refs)])
        return acc[:HEAD_DIM] / acc[HEAD_DIM:HEAD_DIM + 1]

    heads = []
    pending = scores(0)
    for idx in range(N_HEADS):
        nxt = scores(idx + 1) if idx + 1 < N_HEADS else None
        heads.append(weighted_values(idx, *pending))
        pending = nxt
    o_ref[0] = jnp.concatenate(heads, axis=0).T.astype(BF16)


def _attention(q, kv_sets, layer=None, cast_weights=()):
    B, N, _ = q.shape
    tq = min(Q_TILE, N)
    nq = N // tq
    n_steps = B * nq
    in_specs = [pl.BlockSpec((1, tq, ATTN_W), lambda b, i: (b, i, 0))]
    for k, _ in kv_sets:
        in_specs.append(pl.BlockSpec((1, N_KV_HEADS, k.shape[2], KV_W), lambda b, i: (b, 0, 0, 0)))
    for _, vt in kv_sets:
        in_specs.append(pl.BlockSpec((1, N_KV_HEADS, 2 * HEAD_DIM, vt.shape[3]),
                                     lambda b, i: (b, 0, 0, 0)))
    out_specs = [pl.BlockSpec((1, tq, ATTN_W), lambda b, i: (b, i, 0))]
    out_shape = [jax.ShapeDtypeStruct((B, N, ATTN_W), BF16)]
    for w in cast_weights:
        _, rows, cols = w.shape
        per = next(d for d in (1, 2, 4, 8) if rows % (n_steps // d) == 0
                   and (rows // (n_steps // d)) % BF16_ROWS == 0)
        blk = rows // (n_steps // per)
        in_specs.append(pl.BlockSpec((None, blk, cols),
                                     lambda b, i, per=per: (layer, (b * nq + i) // per, 0)))
        out_specs.append(pl.BlockSpec((blk, cols), lambda b, i, per=per: ((b * nq + i) // per, 0)))
        out_shape.append(jax.ShapeDtypeStruct((rows, cols), BF16))
    outs = pl.pallas_call(
        functools.partial(_attn_kernel, n_kv_sets=len(kv_sets), n_cast=len(cast_weights)),
        grid=(B, nq),
        in_specs=in_specs,
        out_specs=out_specs,
        out_shape=out_shape,
        compiler_params=pltpu.CompilerParams(
            dimension_semantics=("arbitrary", "arbitrary"), vmem_limit_bytes=VMEM_LIMIT),
        name=f"attention_{len(kv_sets)}",
    )(q, *[k for k, _ in kv_sets], *[vt for _, vt in kv_sets], *cast_weights)
    return outs if cast_weights else outs[0]


def _mix_stage(i, n_tiles, x_ref, mod_ref, attn_ref, u_ref, vn_ref, p_ref, pprev_ref, pnext_ref,
               ws_ref, bs_ref, pw_ref, ps_ref, wo_ref, n2_ref, seq_len):
    tm = x_ref.shape[1]
    m = mod_ref[0]

    lo = _half_mask((CHUNK, LANES))
    sgu_chunks = []
    for c in range(tm // CHUNK):
        slabs = []
        for j in range(SGU_W // LANES):
            vs = vn_ref[0, c * CHUNK:(c + 1) * CHUNK, j * LANES:(j + 1) * LANES]
            ra = jnp.dot(ws_ref[2 * j], vs, preferred_element_type=F32)
            rb = jnp.dot(ws_ref[2 * j + 1], vs, preferred_element_type=F32)
            slabs.append(jnp.where(lo, ra, rb))
        mixed = jnp.concatenate(slabs, axis=-1) + bs_ref[...]
        sgu_chunks.append((u_ref[0, c * CHUNK:(c + 1) * CHUNK, :] * mixed).astype(BF16))
    sgu = jnp.concatenate(sgu_chunks, axis=0)

    p = p_ref[0]
    prev = jnp.where(i > 0, pprev_ref[0], 0.0)
    nxt = jnp.where(i < n_tiles - 1, pnext_ref[0], 0.0)
    e = jnp.concatenate([prev, p, nxt], axis=0)
    n_ext = tm + 2 * POOL_HALO

    def ahead(a, k):
        return pltpu.roll(a, n_ext - k, axis=0)

    a2 = e + ahead(e, 1)
    a4 = a2 + ahead(a2, 2)
    a8 = a4 + ahead(a4, 4)
    a16 = a8 + ahead(a8, 8)
    s2 = ahead(a2, POOL_HALO - 1)[:tm]
    s4 = ahead(a4, POOL_HALO - 2)[:tm]
    s8 = ahead(a8, POOL_HALO - 4)[:tm]
    s16 = a16[:tm]
    grp = lax.broadcasted_iota(jnp.int32, (tm, POOL_W), 1) >> 6
    win_sum = jnp.where(grp == 0, s2, jnp.where(grp == 1, s4, jnp.where(grp == 2, s8, s16)))
    t = i * tm + lax.broadcasted_iota(jnp.int32, (tm, POOL_W), 0)
    half = jnp.left_shift(1, grp)
    cnt = jnp.minimum(t + half, seq_len) - jnp.maximum(t - half, 0)
    d = win_sum / cnt.astype(F32) - p
    pool = jnp.dot(d.astype(BF16), pw_ref[...], preferred_element_type=F32) * ps_ref[...]


    cat = jnp.concatenate([attn_ref[0], sgu, pool.astype(BF16)], axis=-1)
    mix = jnp.dot(cat, wo_ref[...], preferred_element_type=F32)
    x1 = x_ref[0] + m[2:3] * mix
    h2 = (_rms(x1, n2_ref[...]) * (1.0 + m[4:5]) + m[3:4]).astype(BF16)
    return x1, h2


def _ffn_stage(x1, h2, gate2, wg_ref, wu_ref, wd_ref, fn_ref, final):
    gate = jnp.dot(h2, wg_ref[...], preferred_element_type=F32)
    up = jnp.dot(h2, wu_ref[...], preferred_element_type=F32)
    act = (_silu(gate) * up).astype(BF16)
    x2 = x1 + gate2 * jnp.dot(act, wd_ref[...], preferred_element_type=F32)
    if final:
        x2 = _rms(x2, fn_ref[...])
    return x2


def _mixffn_kernel(x_ref, mod_ref, modp_ref, attn_ref, u_ref, vn_ref, p_ref, pprev_ref, pnext_ref,
                   ws_ref, bs_ref, pw_ref, ps_ref, wo_ref, n2_ref, wg_ref, wu_ref, wd_ref,
                   fn_ref, o_ref, x1_scr, h2_scr, *, seq_len, tiles_per_seq, final):
    s = pl.program_id(0)
    n_steps = pl.num_programs(0)
    i = lax.rem(jnp.minimum(s, n_steps - 2), tiles_per_seq)
    mix_args = (i, tiles_per_seq, x_ref, mod_ref, attn_ref, u_ref, vn_ref, p_ref, pprev_ref,
                pnext_ref, ws_ref, bs_ref, pw_ref, ps_ref, wo_ref, n2_ref, seq_len)
    ffn_args = (wg_ref, wu_ref, wd_ref, fn_ref, final)

    @pl.when(s == 0)
    def _():
        x1, h2 = _mix_stage(*mix_args)
        x1_scr[...] = x1
        h2_scr[...] = h2

    @pl.when(jnp.logical_and(s > 0, s < n_steps - 1))
    def _():
        o_ref[0] = _ffn_stage(x1_scr[...], h2_scr[...], modp_ref[0][5:6], *ffn_args)
        x1, h2 = _mix_stage(*mix_args)
        x1_scr[...] = x1
        h2_scr[...] = h2

    @pl.when(s == n_steps - 1)
    def _():
        o_ref[0] = _ffn_stage(x1_scr[...], h2_scr[...], modp_ref[0][5:6], *ffn_args)


def _mixffn(layer, x, mod, mod_row_of_batch, attn, u, vn, p, ws, bs, pw, ps, wo, n2, wg, wu, wd,
            fn, *, final):
    B, N, _ = x.shape
    tm = min(TOKEN_TILE, N)
    tps = N // tm
    n_tiles = B * tps
    hb = tm // POOL_HALO
    n_hblocks = N // POOL_HALO

    def cur(s):
        t = jnp.minimum(s, n_tiles - 1)
        return t // tps, lax.rem(t, tps)

    def prv(s):
        t = jnp.maximum(s - 1, 0)
        return t // tps, lax.rem(t, tps)

    def tile(w):
        return pl.BlockSpec((1, tm, w), lambda s: (*cur(s), 0))

    def halo_prev(s):
        b, i = cur(s)
        return b, jnp.maximum(i * hb - 1, 0), 0

    def halo_next(s):
        b, i = cur(s)
        return b, jnp.minimum((i + 1) * hb, n_hblocks - 1), 0

    once = dict(pipeline_mode=pl.Buffered(1))
    return pl.pallas_call(
        functools.partial(_mixffn_kernel, seq_len=N, tiles_per_seq=tps, final=final),
        grid=(n_tiles + 1,),
        in_specs=[
            tile(D_MODEL),
            pl.BlockSpec((None, 1, 6, D_MODEL),
                         lambda s: (layer, mod_row_of_batch(cur(s)[0]), 0, 0)),
            pl.BlockSpec((None, 1, 6, D_MODEL),
                         lambda s: (layer, mod_row_of_batch(prv(s)[0]), 0, 0)),
            tile(ATTN_W), tile(SGU_W), tile(SGU_W), tile(POOL_W),
            pl.BlockSpec((1, POOL_HALO, POOL_W), halo_prev),
            pl.BlockSpec((1, POOL_HALO, POOL_W), halo_next),
            _layer_spec(layer, SGU_GROUPS, CHUNK, CHUNK, **once),
            _layer_spec(layer, CHUNK, SGU_W, **once),
            _layer_spec(layer, POOL_W, POOL_W, **once),
            _layer_spec(layer, 1, POOL_W, **once),
            pl.BlockSpec((MIX_W, D_MODEL), lambda s: (0, 0), **once),
            _layer_spec(layer, 1, D_MODEL, **once),
            pl.BlockSpec((D_MODEL, D_FF), lambda s: (0, 0), **once),
            pl.BlockSpec((D_MODEL, D_FF), lambda s: (0, 0), **once),
            pl.BlockSpec((D_FF, D_MODEL), lambda s: (0, 0), **once),
            pl.BlockSpec((1, D_MODEL), lambda s: (0, 0), **once),
        ],
        out_specs=pl.BlockSpec((1, tm, D_MODEL), lambda s: (*prv(s), 0)),
        out_shape=jax.ShapeDtypeStruct((B, N, D_MODEL), F32),
        scratch_shapes=[pltpu.VMEM((tm, D_MODEL), F32), pltpu.VMEM((tm, D_MODEL), BF16)],
        compiler_params=pltpu.CompilerParams(
            dimension_semantics=("arbitrary",), vmem_limit_bytes=VMEM_LIMIT),
        name="mixffn_final" if final else "mixffn",
    )(x, mod, mod, attn, u, vn, p, p, p, ws, bs, pw, ps, wo, n2, wg, wu, wd, fn)


def _rope_tables(n_tokens):
    rows = jnp.repeat(jnp.arange(n_tokens // GRID_W, dtype=F32), GRID_W)
    cols = jnp.tile(jnp.arange(GRID_W, dtype=F32), n_tokens // GRID_W)
    inv = ROPE_THETA ** (-jnp.arange(ROPE_PAIRS_PER_AXIS, dtype=F32) / ROPE_PAIRS_PER_AXIS)
    ang = jnp.concatenate([rows[:, None] * inv, cols[:, None] * inv], axis=-1)
    cos, sin = jnp.cos(ang), jnp.sin(ang)
    return jnp.tile(cos, (1, 4)), jnp.tile(jnp.concatenate([-sin, sin], axis=-1), (1, 2))


def _block_diag(w):
    d, g, a, b = w.shape
    eye = jnp.eye(g, dtype=w.dtype)
    return (w[:, :, :, None, :] * eye[None, :, None, :, None]).reshape(d, g * a, g * b)


def kernel(x, c, ctx, c_ctx, w_mod, b_mod, norm1, norm2, w_in, q_norm, k_norm, sgu_norm, w_s, b_s,
           pool_w, pool_scale, w_out, w_gate, w_up, w_down, final_norm):
    B, S, _ = x.shape
    L = ctx.shape[1]
    c_rows = jnp.concatenate(
        [c, c_ctx[None, :], jnp.zeros((MOD_ROWS - B - 1, D_MODEL), F32)], axis=0)
    mod, w_in_bf = _modulation(c_rows, w_mod, b_mod, w_in)
    mod = mod.reshape(DEPTH, MOD_ROWS, 6, D_MODEL)
    cos, sin = _rope_tables(S)
    ones = jnp.ones((L, LANES), F32)

    row = lambda a: a[:, None, :]
    proj_w = (
        row(norm1),
        w_in_bf,
        row(jnp.tile(q_norm, (1, 2))),
        row(jnp.tile(k_norm, (1, 2))),
        row(sgu_norm),
    )
    ws_bf = w_s.astype(BF16)
    sgu_bias = jnp.repeat(jnp.swapaxes(b_s, 1, 2), SGU_W // SGU_GROUPS, axis=2)
    pool_bd = _block_diag(pool_w).astype(BF16)
    lat_row = lambda b: b
    ctx_row = lambda b: B

    for l in range(DEPTH):
        last = l == DEPTH - 1
        q, k, vt, u, vn, p = _inproj(l, x, mod, lat_row, *proj_w, cos, sin, rope=True)
        qc, kc, vtc, uc, vnc, pc = _inproj(l, ctx, mod, ctx_row, *proj_w, ones, ones, rope=False)
        attn, wo_bf, wg_bf, wu_bf, wd_bf = _attention(
            q, [(kc, vtc), (k, vt)], layer=l, cast_weights=(w_out, w_gate, w_up, w_down))
        mix_w = (ws_bf, sgu_bias, pool_bd, row(pool_scale), wo_bf, row(norm2), wg_bf, wu_bf, wd_bf,
                 final_norm[None, :])
        if not last:
            attn_c = _attention(qc, [(kc, vtc)])
            ctx = _mixffn(l, ctx, mod, ctx_row, attn_c, uc, vnc, pc, *mix_w, final=False)
        x = _mixffn(l, x, mod, lat_row, attn, u, vn, p, *mix_w, final=last)
    return x
```

```python
import functools
import math

import jax
import jax.numpy as jnp
from jax import lax
from jax.experimental import pallas as pl
from jax.experimental.pallas import tpu as pltpu

F32 = jnp.float32
BF16 = jnp.bfloat16

D_MODEL = 1024
DEPTH = 2
GRID_W = 64
EPS = 1e-6

HEAD_DIM = 64
N_HEADS = 8
N_KV_HEADS = 2
Q_PER_KV = N_HEADS // N_KV_HEADS
ATTN_W = N_HEADS * HEAD_DIM
KV_W = N_KV_HEADS * HEAD_DIM
ROPE_THETA = 10000.0
ROPE_PAIRS_PER_AXIS = HEAD_DIM // 4
ATTN_SCALE = HEAD_DIM ** -0.5
LOG2E = math.log2(math.e)
BOUND_MARGIN = 1.02
MAX_SCORE_BOUND = 50.0

CHUNK = 128
SGU_W = D_MODEL // 4
SGU_GROUPS = 4
POOL_WINDOWS = (2, 4, 8, 16)
POOL_W = D_MODEL // 4
POOL_HALO = 8

MIX_W = ATTN_W + SGU_W + POOL_W
O_Q = 0
O_K = O_Q + ATTN_W
O_V = O_K + KV_W
O_G = O_V + KV_W
O_P = O_G + 2 * SGU_W
IN_W = O_P + POOL_W
D_FF = 2816

LANES = 128
BF16_ROWS = 16
MOD_ROWS = 16
VMEM_LIMIT = 56 * 1024 * 1024

TOKEN_TILE = 256
INPROJ_TILE = 1024
SUB_TILE = 256
Q_TILE = 512


def _half_mask(shape):
    lane = lax.broadcasted_iota(jnp.int32, shape, len(shape) - 1)
    return (lane & HEAD_DIM) == 0


def _rms(x, gain):
    return x * lax.rsqrt(jnp.mean(x * x, axis=-1, keepdims=True) + EPS) * gain


def _head_rms_slab(z, gain):
    lo = _half_mask(z.shape)
    sq = z * z
    s0 = jnp.sum(jnp.where(lo, sq, 0.0), axis=-1, keepdims=True)
    s1 = jnp.sum(jnp.where(lo, 0.0, sq), axis=-1, keepdims=True)
    ms = jnp.where(lo, s0, s1) * (1.0 / HEAD_DIM)
    return z * lax.rsqrt(ms + EPS) * gain


def _rope_slab(x, cos, sin_signed):
    lane = lax.broadcasted_iota(jnp.int32, x.shape, 1)
    first = (lane & (HEAD_DIM // 2)) == 0
    up = pltpu.roll(x, LANES - HEAD_DIM // 2, axis=1)
    down = pltpu.roll(x, HEAD_DIM // 2, axis=1)
    partner = jnp.where(first, up, down)
    return x * cos + partner * sin_signed


def _gelu_tanh(x):
    c = math.sqrt(2.0 / math.pi)
    return 0.5 * x * (1.0 + jnp.tanh(c * (x + 0.044715 * (x * x * x))))


def _silu(x):
    return x / (1.0 + jnp.exp(-x))


def _mod_kernel(c_ref, w_ref, b_ref, win_ref, o_ref, win_bf_ref):
    win_bf_ref[...] = win_ref[...].astype(BF16)

    a = _silu(c_ref[...])
    w = w_ref[0]
    a_hi = a.astype(BF16)
    a_lo = (a - a_hi.astype(F32)).astype(BF16)
    w_hi = w.astype(BF16)
    w_lo = (w - w_hi.astype(F32)).astype(BF16)
    dot = functools.partial(jnp.dot, preferred_element_type=F32)
    o_ref[0] = dot(a_hi, w_hi) + (dot(a_lo, w_hi) + dot(a_hi, w_lo)) + b_ref[0]


def _modulation(c_rows, w_mod, b_mod, w_in):
    n_col = 6
    win_cols = IN_W // n_col
    win_spec = pl.BlockSpec((None, D_MODEL, win_cols), lambda l, j: (l, 0, j))
    return pl.pallas_call(
        _mod_kernel,
        grid=(DEPTH, n_col),
        in_specs=[
            pl.BlockSpec((MOD_ROWS, D_MODEL), lambda l, j: (0, 0)),
            pl.BlockSpec((1, D_MODEL, D_MODEL), lambda l, j: (l, 0, j)),
            pl.BlockSpec((1, 1, D_MODEL), lambda l, j: (l, 0, j)),
            win_spec,
        ],
        out_specs=[pl.BlockSpec((1, MOD_ROWS, D_MODEL), lambda l, j: (l, 0, j)), win_spec],
        out_shape=[jax.ShapeDtypeStruct((DEPTH, MOD_ROWS, 6 * D_MODEL), F32),
                   jax.ShapeDtypeStruct(w_in.shape, BF16)],
        compiler_params=pltpu.CompilerParams(dimension_semantics=("parallel", "parallel")),
        name="modulation",
    )(c_rows, w_mod, b_mod.reshape(DEPTH, 1, 6 * D_MODEL), w_in)


def _inproj_kernel(x_ref, mod_ref, n1_ref, w_ref, qg_ref, kg_ref, sg_ref, cos_ref, sin_ref,
                   q_ref, k_ref, vt_ref, u_ref, vn_ref, p_ref, *, rope):
    m = mod_ref[0]
    tm = x_ref.shape[1]
    sub = min(SUB_TILE, tm)

    def project(r0):
        rows = slice(r0, r0 + sub)
        h = _rms(x_ref[0, rows, :], n1_ref[...]) * (1.0 + m[1:2]) + m[0:1]
        return jnp.dot(h.astype(BF16), w_ref[...], preferred_element_type=F32)

    def finish(r0, z):
        rows = slice(r0, r0 + sub)
        if rope:
            cos = cos_ref[rows, :]
            sin = sin_ref[rows, :]

        for j in range(ATTN_W // LANES):
            zq = z[:, O_Q + j * LANES:O_Q + (j + 1) * LANES]
            qn = _head_rms_slab(zq, qg_ref[...])
            if rope:
                qn = _rope_slab(qn, cos, sin)
            q_ref[0, rows, j * LANES:(j + 1) * LANES] = (qn * (ATTN_SCALE * LOG2E)).astype(BF16)

        kn = _head_rms_slab(z[:, O_K:O_K + KV_W], kg_ref[...])
        if rope:
            kn = _rope_slab(kn, cos, sin)
        k_swapped = pltpu.roll(kn, HEAD_DIM, axis=1)
        lo = _half_mask(kn.shape)
        k_ref[0, 0, rows, :] = jnp.where(lo, kn, k_swapped).astype(BF16)
        k_ref[0, 1, rows, :] = jnp.where(lo, k_swapped, kn).astype(BF16)

        vt = z[:, O_V:O_V + KV_W].T
        ones = jnp.ones((HEAD_DIM, sub), F32)
        for hh in range(N_KV_HEADS):
            vt_ref[0, hh, :, rows] = jnp.concatenate(
                [vt[hh * HEAD_DIM:(hh + 1) * HEAD_DIM], ones], axis=0).astype(BF16)

        g = _gelu_tanh(z[:, O_G:O_P])
        u_ref[0, rows, :] = g[:, :SGU_W]
        vn_ref[0, rows, :] = _rms(g[:, SGU_W:], sg_ref[...]).astype(BF16)

        p_ref[0, rows, :] = z[:, O_P:]

    starts = list(range(0, tm, sub))
    z = project(starts[0])
    for r0, r1 in zip(starts, starts[1:] + [None]):
        z_next = project(r1) if r1 is not None else None
        finish(r0, z)
        z = z_next


def _layer_spec(layer, *shape, **kwargs):
    return pl.BlockSpec((None, *shape), lambda *_: (layer,) + (0,) * len(shape), **kwargs)


def _inproj(layer, x, mod, mod_row_of_batch, norm1, w_in, qg, kg, sg, cos, sin, *, rope):
    B, N, _ = x.shape
    tm = min(INPROJ_TILE, N)
    grid = (B, N // tm)
    return pl.pallas_call(
        functools.partial(_inproj_kernel, rope=rope),
        grid=grid,
        in_specs=[
            pl.BlockSpec((1, tm, D_MODEL), lambda b, i: (b, i, 0)),
            pl.BlockSpec((None, 1, 6, D_MODEL), lambda b, i: (layer, mod_row_of_batch(b), 0, 0)),
            _layer_spec(layer, 1, D_MODEL),
            _layer_spec(layer, D_MODEL, IN_W),
            _layer_spec(layer, 1, LANES),
            _layer_spec(layer, 1, LANES),
            _layer_spec(layer, 1, SGU_W),
            pl.BlockSpec((tm, LANES), lambda b, i: (i, 0)),
            pl.BlockSpec((tm, LANES), lambda b, i: (i, 0)),
        ],
        out_specs=[
            pl.BlockSpec((1, tm, ATTN_W), lambda b, i: (b, i, 0)),
            pl.BlockSpec((1, N_KV_HEADS, tm, KV_W), lambda b, i: (b, 0, i, 0)),
            pl.BlockSpec((1, N_KV_HEADS, 2 * HEAD_DIM, tm), lambda b, i: (b, 0, 0, i)),
            pl.BlockSpec((1, tm, SGU_W), lambda b, i: (b, i, 0)),
            pl.BlockSpec((1, tm, SGU_W), lambda b, i: (b, i, 0)),
            pl.BlockSpec((1, tm, POOL_W), lambda b, i: (b, i, 0)),
        ],
        out_shape=[
            jax.ShapeDtypeStruct((B, N, ATTN_W), BF16),
            jax.ShapeDtypeStruct((B, N_KV_HEADS, N, KV_W), BF16),
            jax.ShapeDtypeStruct((B, N_KV_HEADS, 2 * HEAD_DIM, N), BF16),
            jax.ShapeDtypeStruct((B, N, SGU_W), F32),
            jax.ShapeDtypeStruct((B, N, SGU_W), BF16),
            jax.ShapeDtypeStruct((B, N, POOL_W), F32),
        ],
        compiler_params=pltpu.CompilerParams(
            dimension_semantics=("parallel", "parallel"), vmem_limit_bytes=VMEM_LIMIT),
        name="inproj_rope" if rope else "inproj",
    )(x, mod, norm1, w_in, qg, kg, sg, cos, sin)


def _attn_kernel(*refs, n_kv_sets, n_cast):
    bound_ref, q_ref = refs[:2]
    refs = refs[1:]
    k_refs = refs[1:1 + n_kv_sets]
    vt_refs = refs[1 + n_kv_sets:1 + 2 * n_kv_sets]
    n_in = 1 + 2 * n_kv_sets + n_cast
    o_ref = refs[n_in]
    for src, dst in zip(refs[n_in - n_cast:n_in], refs[n_in + 1:]):
        dst[...] = src[...].astype(BF16)
    tq = q_ref.shape[1]
    lo = _half_mask((tq, LANES))
    zero = jnp.zeros((tq, LANES), BF16)
    nt = (((1,), (1,)), ((), ()))

    def attend(bound):
        def scores(idx):
            slab, half, kvh = idx // 2, idx % 2, idx // Q_PER_KV
            qs = q_ref[0, :, slab * LANES:(slab + 1) * LANES]
            qm = jnp.where(lo, qs, zero) if half == 0 else jnp.where(lo, zero, qs)
            st = [lax.dot_general(k[0, kvh], qm, nt, preferred_element_type=F32) for k in k_refs]
            if bound is not None:
                return st, bound
            return st, functools.reduce(
                jnp.maximum, [jnp.max(si, axis=0, keepdims=True) for si in st])

        def weighted_values(idx, st, shift):
            kvh = idx // Q_PER_KV
            acc = functools.reduce(lambda a, b: a + b, [
                jnp.dot(vt[0, kvh], jnp.exp2(si - shift).astype(BF16), preferred_element_type=F32)
                for si, vt in zip(st, vt_refs)])
            return acc[:HEAD_DIM] / acc[HEAD_DIM:HEAD_DIM + 1]

        heads = []
        pending = scores(0)
        for idx in range(N_HEADS):
            nxt = scores(idx + 1) if idx + 1 < N_HEADS else None
            heads.append(weighted_values(idx, *pending))
            pending = nxt
        o_ref[0] = jnp.concatenate(heads, axis=0).T.astype(BF16)

    bound = bound_ref[0, 0]
    safe = bound <= MAX_SCORE_BOUND

    @pl.when(safe)
    def _():
        attend(bound)

    @pl.when(jnp.logical_not(safe))
    def _():
        attend(None)


def _attention(score_bound, q, kv_sets, layer=None, cast_weights=()):
    B, N, _ = q.shape
    tq = min(Q_TILE, N)
    nq = N // tq
    n_steps = B * nq
    in_specs = [pl.BlockSpec(memory_space=pltpu.SMEM),
                pl.BlockSpec((1, tq, ATTN_W), lambda b, i: (b, i, 0))]
    for k, _ in kv_sets:
        in_specs.append(pl.BlockSpec((1, N_KV_HEADS, k.shape[2], KV_W), lambda b, i: (b, 0, 0, 0)))
    for _, vt in kv_sets:
        in_specs.append(pl.BlockSpec((1, N_KV_HEADS, 2 * HEAD_DIM, vt.shape[3]),
                                     lambda b, i: (b, 0, 0, 0)))
    out_specs = [pl.BlockSpec((1, tq, ATTN_W), lambda b, i: (b, i, 0))]
    out_shape = [jax.ShapeDtypeStruct((B, N, ATTN_W), BF16)]
    for w in cast_weights:
        _, rows, cols = w.shape
        per = next(d for d in (1, 2, 4, 8) if rows % (n_steps // d) == 0
                   and (rows // (n_steps // d)) % BF16_ROWS == 0)
        blk = rows // (n_steps // per)
        in_specs.append(pl.BlockSpec((None, blk, cols),
                                     lambda b, i, per=per: (layer, (b * nq + i) // per, 0)))
        out_specs.append(pl.BlockSpec((blk, cols), lambda b, i, per=per: ((b * nq + i) // per, 0)))
        out_shape.append(jax.ShapeDtypeStruct((rows, cols), BF16))
    outs = pl.pallas_call(
        functools.partial(_attn_kernel, n_kv_sets=len(kv_sets), n_cast=len(cast_weights)),
        grid=(B, nq),
        in_specs=in_specs,
        out_specs=out_specs,
        out_shape=out_shape,
        compiler_params=pltpu.CompilerParams(
            dimension_semantics=("arbitrary", "arbitrary"), vmem_limit_bytes=VMEM_LIMIT),
        name=f"attention_{len(kv_sets)}",
    )(score_bound, q, *[k for k, _ in kv_sets], *[vt for _, vt in kv_sets], *cast_weights)
    return outs if cast_weights else outs[0]


def _mix_stage(i, n_tiles, x_ref, mod_ref, attn_ref, u_ref, vn_ref, p_ref, pprev_ref, pnext_ref,
               ws_ref, bs_ref, pw_ref, ps_ref, wo_ref, n2_ref, seq_len):
    tm = x_ref.shape[1]
    m = mod_ref[0]

    lo = _half_mask((CHUNK, LANES))
    sgu_chunks = []
    for c in range(tm // CHUNK):
        slabs = []
        for j in range(SGU_W // LANES):
            vs = vn_ref[0, c * CHUNK:(c + 1) * CHUNK, j * LANES:(j + 1) * LANES]
            ra = jnp.dot(ws_ref[2 * j], vs, preferred_element_type=F32)
            rb = jnp.dot(ws_ref[2 * j + 1], vs, preferred_element_type=F32)
            slabs.append(jnp.where(lo, ra, rb))
        mixed = jnp.concatenate(slabs, axis=-1) + bs_ref[...]
        sgu_chunks.append((u_ref[0, c * CHUNK:(c + 1) * CHUNK, :] * mixed).astype(BF16))
    sgu = jnp.concatenate(sgu_chunks, axis=0)

    p = p_ref[0]
    prev = jnp.where(i > 0, pprev_ref[0], 0.0)
    nxt = jnp.where(i < n_tiles - 1, pnext_ref[0], 0.0)
    e = jnp.concatenate([prev, p, nxt], axis=0)
    n_ext = tm + 2 * POOL_HALO

    def ahead(a, k):
        return pltpu.roll(a, n_ext - k, axis=0)

    a2 = e + ahead(e, 1)
    a4 = a2 + ahead(a2, 2)
    a8 = a4 + ahead(a4, 4)
    a16 = a8 + ahead(a8, 8)
    s2 = ahead(a2, POOL_HALO - 1)[:tm]
    s4 = ahead(a4, POOL_HALO - 2)[:tm]
    s8 = ahead(a8, POOL_HALO - 4)[:tm]
    s16 = a16[:tm]
    grp = lax.broadcasted_iota(jnp.int32, (tm, POOL_W), 1) >> 6
    win_sum = jnp.where(grp == 0, s2, jnp.where(grp == 1, s4, jnp.where(grp == 2, s8, s16)))
    t = i * tm + lax.broadcasted_iota(jnp.int32, (tm, POOL_W), 0)
    half = jnp.left_shift(1, grp)
    cnt = jnp.minimum(t + half, seq_len) - jnp.maximum(t - half, 0)
    d = win_sum / cnt.astype(F32) - p
    pool = jnp.dot(d.astype(BF16), pw_ref[...], preferred_element_type=F32) * ps_ref[...]


    cat = jnp.concatenate([attn_ref[0], sgu, pool.astype(BF16)], axis=-1)
    mix = jnp.dot(cat, wo_ref[...], preferred_element_type=F32)
    x1 = x_ref[0] + m[2:3] * mix
    h2 = (_rms(x1, n2_ref[...]) * (1.0 + m[4:5]) + m[3:4]).astype(BF16)
    return x1, h2


def _ffn_stage(x1, h2, gate2, wg_ref, wu_ref, wd_ref, fn_ref, final):
    gate = jnp.dot(h2, wg_ref[...], preferred_element_type=F32)
    up = jnp.dot(h2, wu_ref[...], preferred_element_type=F32)
    act = (_silu(gate) * up).astype(BF16)
    x2 = x1 + gate2 * jnp.dot(act, wd_ref[...], preferred_element_type=F32)
    if final:
        x2 = _rms(x2, fn_ref[...])
    return x2


def _mixffn_kernel(x_ref, mod_ref, modp_ref, attn_ref, u_ref, vn_ref, p_ref, pprev_ref, pnext_ref,
                   ws_ref, bs_ref, pw_ref, ps_ref, wo_ref, n2_ref, wg_ref, wu_ref, wd_ref,
                   fn_ref, o_ref, x1_scr, h2_scr, *, seq_len, tiles_per_seq, final):
    s = pl.program_id(0)
    n_steps = pl.num_programs(0)
    i = lax.rem(jnp.minimum(s, n_steps - 2), tiles_per_seq)
    mix_args = (i, tiles_per_seq, x_ref, mod_ref, attn_ref, u_ref, vn_ref, p_ref, pprev_ref,
                pnext_ref, ws_ref, bs_ref, pw_ref, ps_ref, wo_ref, n2_ref, seq_len)
    ffn_args = (wg_ref, wu_ref, wd_ref, fn_ref, final)

    @pl.when(s == 0)
    def _():
        x1, h2 = _mix_stage(*mix_args)
        x1_scr[...] = x1
        h2_scr[...] = h2

    @pl.when(jnp.logical_and(s > 0, s < n_steps - 1))
    def _():
        o_ref[0] = _ffn_stage(x1_scr[...], h2_scr[...], modp_ref[0][5:6], *ffn_args)
        x1, h2 = _mix_stage(*mix_args)
        x1_scr[...] = x1
        h2_scr[...] = h2

    @pl.when(s == n_steps - 1)
    def _():
        o_ref[0] = _ffn_stage(x1_scr[...], h2_scr[...], modp_ref[0][5:6], *ffn_args)


def _mixffn(layer, x, mod, mod_row_of_batch, attn, u, vn, p, ws, bs, pw, ps, wo, n2, wg, wu, wd,
            fn, *, final):
    B, N, _ = x.shape
    tm = min(TOKEN_TILE, N)
    tps = N // tm
    n_tiles = B * tps
    hb = tm // POOL_HALO
    n_hblocks = N // POOL_HALO

    def cur(s):
        t = jnp.minimum(s, n_tiles - 1)
        return t // tps, lax.rem(t, tps)

    def prv(s):
        t = jnp.maximum(s - 1, 0)
        return t // tps, lax.rem(t, tps)

    def tile(w):
        return pl.BlockSpec((1, tm, w), lambda s: (*cur(s), 0))

    def halo_prev(s):
        b, i = cur(s)
        return b, jnp.maximum(i * hb - 1, 0), 0

    def halo_next(s):
        b, i = cur(s)
        return b, jnp.minimum((i + 1) * hb, n_hblocks - 1), 0

    once = dict(pipeline_mode=pl.Buffered(1))
    return pl.pallas_call(
        functools.partial(_mixffn_kernel, seq_len=N, tiles_per_seq=tps, final=final),
        grid=(n_tiles + 1,),
        in_specs=[
            tile(D_MODEL),
            pl.BlockSpec((None, 1, 6, D_MODEL),
                         lambda s: (layer, mod_row_of_batch(cur(s)[0]), 0, 0)),
            pl.BlockSpec((None, 1, 6, D_MODEL),
                         lambda s: (layer, mod_row_of_batch(prv(s)[0]), 0, 0)),
            tile(ATTN_W), tile(SGU_W), tile(SGU_W), tile(POOL_W),
            pl.BlockSpec((1, POOL_HALO, POOL_W), halo_prev),
            pl.BlockSpec((1, POOL_HALO, POOL_W), halo_next),
            _layer_spec(layer, SGU_GROUPS, CHUNK, CHUNK, **once),
            _layer_spec(layer, CHUNK, SGU_W, **once),
            _layer_spec(layer, POOL_W, POOL_W, **once),
            _layer_spec(layer, 1, POOL_W, **once),
            pl.BlockSpec((MIX_W, D_MODEL), lambda s: (0, 0), **once),
            _layer_spec(layer, 1, D_MODEL, **once),
            pl.BlockSpec((D_MODEL, D_FF), lambda s: (0, 0), **once),
            pl.BlockSpec((D_MODEL, D_FF), lambda s: (0, 0), **once),
            pl.BlockSpec((D_FF, D_MODEL), lambda s: (0, 0), **once),
            pl.BlockSpec((1, D_MODEL), lambda s: (0, 0), **once),
        ],
        out_specs=pl.BlockSpec((1, tm, D_MODEL), lambda s: (*prv(s), 0)),
        out_shape=jax.ShapeDtypeStruct((B, N, D_MODEL), F32),
        scratch_shapes=[pltpu.VMEM((tm, D_MODEL), F32), pltpu.VMEM((tm, D_MODEL), BF16)],
        compiler_params=pltpu.CompilerParams(
            dimension_semantics=("arbitrary",), vmem_limit_bytes=VMEM_LIMIT),
        name="mixffn_final" if final else "mixffn",
    )(x, mod, mod, attn, u, vn, p, p, p, ws, bs, pw, ps, wo, n2, wg, wu, wd, fn)


def _rope_tables(n_tokens):
    rows = jnp.repeat(jnp.arange(n_tokens // GRID_W, dtype=F32), GRID_W)
    cols = jnp.tile(jnp.arange(GRID_W, dtype=F32), n_tokens // GRID_W)
    inv = ROPE_THETA ** (-jnp.arange(ROPE_PAIRS_PER_AXIS, dtype=F32) / ROPE_PAIRS_PER_AXIS)
    ang = jnp.concatenate([rows[:, None] * inv, cols[:, None] * inv], axis=-1)
    cos, sin = jnp.cos(ang), jnp.sin(ang)
    return jnp.tile(cos, (1, 4)), jnp.tile(jnp.concatenate([-sin, sin], axis=-1), (1, 2))


def _block_diag(w):
    d, g, a, b = w.shape
    eye = jnp.eye(g, dtype=w.dtype)
    return (w[:, :, :, None, :] * eye[None, :, None, :, None]).reshape(d, g * a, g * b)


def kernel(x, c, ctx, c_ctx, w_mod, b_mod, norm1, norm2, w_in, q_norm, k_norm, sgu_norm, w_s, b_s,
           pool_w, pool_scale, w_out, w_gate, w_up, w_down, final_norm):
    B, S, _ = x.shape
    L = ctx.shape[1]
    c_rows = jnp.concatenate(
        [c, c_ctx[None, :], jnp.zeros((MOD_ROWS - B - 1, D_MODEL), F32)], axis=0)
    mod, w_in_bf = _modulation(c_rows, w_mod, b_mod, w_in)
    mod = mod.reshape(DEPTH, MOD_ROWS, 6, D_MODEL)
    cos, sin = _rope_tables(S)
    ones = jnp.ones((L, LANES), F32)

    row = lambda a: a[:, None, :]
    proj_w = (
        row(norm1),
        w_in_bf,
        row(jnp.tile(q_norm, (1, 2))),
        row(jnp.tile(k_norm, (1, 2))),
        row(sgu_norm),
    )
    ws_bf = w_s.astype(BF16)
    sgu_bias = jnp.repeat(jnp.swapaxes(b_s, 1, 2), SGU_W // SGU_GROUPS, axis=2)
    pool_bd = _block_diag(pool_w).astype(BF16)
    lat_row = lambda b: b
    ctx_row = lambda b: B

    for l in range(DEPTH):
        last = l == DEPTH - 1
        q, k, vt, u, vn, p = _inproj(l, x, mod, lat_row, *proj_w, cos, sin, rope=True)
        qc, kc, vtc, uc, vnc, pc = _inproj(l, ctx, mod, ctx_row, *proj_w, ones, ones, rope=False)
        bound = (HEAD_DIM * ATTN_SCALE * LOG2E * BOUND_MARGIN
                 * jnp.max(jnp.abs(q_norm[l])) * jnp.max(jnp.abs(k_norm[l]))).reshape(1, 1)
        attn, wo_bf, wg_bf, wu_bf, wd_bf = _attention(
            bound, q, [(kc, vtc), (k, vt)], layer=l, cast_weights=(w_out, w_gate, w_up, w_down))
        mix_w = (ws_bf, sgu_bias, pool_bd, row(pool_scale), wo_bf, row(norm2), wg_bf, wu_bf, wd_bf,
                 final_norm[None, :])
        if not last:
            attn_c = _attention(bound, qc, [(kc, vtc)])
            ctx = _mixffn(l, ctx, mod, ctx_row, attn_c, uc, vnc, pc, *mix_w, final=False)
        x = _mixffn(l, x, mod, lat_row, attn, u, vn, p, *mix_w, final=last)
    return x
```

```python
import functools
import math

import jax
import jax.numpy as jnp
from jax import lax
from jax.experimental import pallas as pl
from jax.experimental.pallas import tpu as pltpu

F32 = jnp.float32
BF16 = jnp.bfloat16

D_MODEL = 1024
DEPTH = 2
GRID_W = 64
EPS = 1e-6

HEAD_DIM = 64
N_HEADS = 8
N_KV_HEADS = 2
Q_PER_KV = N_HEADS // N_KV_HEADS
ATTN_W = N_HEADS * HEAD_DIM
KV_W = N_KV_HEADS * HEAD_DIM
ROPE_THETA = 10000.0
ROPE_PAIRS_PER_AXIS = HEAD_DIM // 4
ATTN_SCALE = HEAD_DIM ** -0.5
LOG2E = math.log2(math.e)
BOUND_MARGIN = 1.02
MAX_SCORE_BOUND = 50.0

CHUNK = 128
SGU_W = D_MODEL // 4
SGU_GROUPS = 4
POOL_WINDOWS = (2, 4, 8, 16)
POOL_W = D_MODEL // 4
POOL_HALO = 8

MIX_W = ATTN_W + SGU_W + POOL_W
O_Q = 0
O_K = O_Q + ATTN_W
O_V = O_K + KV_W
O_G = O_V + KV_W
O_P = O_G + 2 * SGU_W
IN_W = O_P + POOL_W
D_FF = 2816

LANES = 128
BF16_ROWS = 16
MOD_ROWS = 16
VMEM_LIMIT = 56 * 1024 * 1024

TOKEN_TILE = 512
INPROJ_TILE = 1024
SUB_TILE = 256
Q_TILE = 512


def _half_mask(shape):
    lane = lax.broadcasted_iota(jnp.int32, shape, len(shape) - 1)
    return (lane & HEAD_DIM) == 0


def _rms(x, gain):
    return x * lax.rsqrt(jnp.mean(x * x, axis=-1, keepdims=True) + EPS) * gain


def _head_rms_slab(z, gain):
    lo = _half_mask(z.shape)
    sq = z * z
    s0 = jnp.sum(jnp.where(lo, sq, 0.0), axis=-1, keepdims=True)
    s1 = jnp.sum(jnp.where(lo, 0.0, sq), axis=-1, keepdims=True)
    r0 = lax.rsqrt(s0 * (1.0 / HEAD_DIM) + EPS)
    r1 = lax.rsqrt(s1 * (1.0 / HEAD_DIM) + EPS)
    return z * jnp.where(lo, r0, r1) * gain


def _rope_slab(x, cos, sin_signed):
    lane = lax.broadcasted_iota(jnp.int32, x.shape, 1)
    first = (lane & (HEAD_DIM // 2)) == 0
    up = pltpu.roll(x, LANES - HEAD_DIM // 2, axis=1)
    down = pltpu.roll(x, HEAD_DIM // 2, axis=1)
    partner = jnp.where(first, up, down)
    return x * cos + partner * sin_signed


def _gelu_tanh(x):
    c = math.sqrt(2.0 / math.pi)
    k0 = -2.0 * c * LOG2E
    k1 = k0 * 0.044715
    return x / (1.0 + jnp.exp2(x * (k1 * (x * x) + k0)))


def _silu(x):
    return x / (1.0 + jnp.exp(-x))


def _mod_kernel(c_ref, w_ref, b_ref, win_ref, o_ref, win_bf_ref):
    win_bf_ref[...] = win_ref[...].astype(BF16)

    a = _silu(c_ref[...])
    w = w_ref[0]
    a_hi = a.astype(BF16)
    a_lo = (a - a_hi.astype(F32)).astype(BF16)
    w_hi = w.astype(BF16)
    w_lo = (w - w_hi.astype(F32)).astype(BF16)
    dot = functools.partial(jnp.dot, preferred_element_type=F32)
    o_ref[0] = dot(a_hi, w_hi) + (dot(a_lo, w_hi) + dot(a_hi, w_lo)) + b_ref[0]


def _modulation(c_rows, w_mod, b_mod, w_in):
    n_col = 6
    win_cols = IN_W // n_col
    win_spec = pl.BlockSpec((None, D_MODEL, win_cols), lambda l, j: (l, 0, j))
    return pl.pallas_call(
        _mod_kernel,
        grid=(DEPTH, n_col),
        in_specs=[
            pl.BlockSpec((MOD_ROWS, D_MODEL), lambda l, j: (0, 0)),
            pl.BlockSpec((1, D_MODEL, D_MODEL), lambda l, j: (l, 0, j)),
            pl.BlockSpec((1, 1, D_MODEL), lambda l, j: (l, 0, j)),
            win_spec,
        ],
        out_specs=[pl.BlockSpec((1, MOD_ROWS, D_MODEL), lambda l, j: (l, 0, j)), win_spec],
        out_shape=[jax.ShapeDtypeStruct((DEPTH, MOD_ROWS, 6 * D_MODEL), F32),
                   jax.ShapeDtypeStruct(w_in.shape, BF16)],
        compiler_params=pltpu.CompilerParams(dimension_semantics=("parallel", "parallel")),
        name="modulation",
    )(c_rows, w_mod, b_mod.reshape(DEPTH, 1, 6 * D_MODEL), w_in)


def _store_kv(kn, v, k_ref, vt_ref, rows):
    k_swapped = pltpu.roll(kn, HEAD_DIM, axis=1)
    lo = _half_mask(kn.shape)
    k_ref[0, 0, rows, :] = jnp.where(lo, kn, k_swapped).astype(BF16)
    k_ref[0, 1, rows, :] = jnp.where(lo, k_swapped, kn).astype(BF16)
    vt = v.T
    ones = jnp.ones((HEAD_DIM, vt.shape[1]), F32)
    for hh in range(N_KV_HEADS):
        vt_ref[0, hh, :, rows] = jnp.concatenate(
            [vt[hh * HEAD_DIM:(hh + 1) * HEAD_DIM], ones], axis=0).astype(BF16)


def _kvproj_kernel(x_ref, mod_ref, n1_ref, w_ref, kg_ref, k_ref, vt_ref):
    m = mod_ref[0]
    h = _rms(x_ref[0], n1_ref[...] * (1.0 + m[1:2])) + m[0:1]
    z = jnp.dot(h.astype(BF16), w_ref[...], preferred_element_type=F32)
    kn = _head_rms_slab(z[:, :KV_W], kg_ref[...])
    _store_kv(kn, z[:, KV_W:], k_ref, vt_ref, slice(None))


def _kvproj(layer, x, mod, mod_row_of_batch, norm1, w_in, kg):
    B, N, _ = x.shape
    tm = min(INPROJ_TILE, N)
    assert O_K % (2 * KV_W) == 0 and O_V == O_K + KV_W
    return pl.pallas_call(
        _kvproj_kernel,
        grid=(B, N // tm),
        in_specs=[
            pl.BlockSpec((1, tm, D_MODEL), lambda b, i: (b, i, 0)),
            pl.BlockSpec((None, 1, 6, D_MODEL), lambda b, i: (layer, mod_row_of_batch(b), 0, 0)),
            _layer_spec(layer, 1, D_MODEL),
            pl.BlockSpec((None, D_MODEL, 2 * KV_W), lambda b, i: (layer, 0, O_K // (2 * KV_W))),
            _layer_spec(layer, 1, LANES),
        ],
        out_specs=[
            pl.BlockSpec((1, N_KV_HEADS, tm, KV_W), lambda b, i: (b, 0, i, 0)),
            pl.BlockSpec((1, N_KV_HEADS, 2 * HEAD_DIM, tm), lambda b, i: (b, 0, 0, i)),
        ],
        out_shape=[
            jax.ShapeDtypeStruct((B, N_KV_HEADS, N, KV_W), BF16),
            jax.ShapeDtypeStruct((B, N_KV_HEADS, 2 * HEAD_DIM, N), BF16),
        ],
        compiler_params=pltpu.CompilerParams(
            dimension_semantics=("parallel", "parallel"), vmem_limit_bytes=VMEM_LIMIT),
        name="kvproj",
    )(x, mod, norm1, w_in, kg)

def _inproj_kernel(x_ref, mod_ref, n1_ref, w_ref, qg_ref, kg_ref, sg_ref, cos_ref, sin_ref,
                   q_ref, k_ref, vt_ref, u_ref, vn_ref, p_ref, *, rope):
    m = mod_ref[0]
    tm = x_ref.shape[1]
    sub = min(SUB_TILE, tm)

    gain1 = n1_ref[...] * (1.0 + m[1:2])
    q_gain = qg_ref[...] * (ATTN_SCALE * LOG2E)

    def project(r0):
        rows = slice(r0, r0 + sub)
        h = _rms(x_ref[0, rows, :], gain1) + m[0:1]
        return jnp.dot(h.astype(BF16), w_ref[...], preferred_element_type=F32)

    def finish(r0, z):
        rows = slice(r0, r0 + sub)
        if rope:
            cos = cos_ref[rows, :]
            sin = sin_ref[rows, :]

        for j in range(ATTN_W // LANES):
            zq = z[:, O_Q + j * LANES:O_Q + (j + 1) * LANES]
            qn = _head_rms_slab(zq, q_gain)
            if rope:
                qn = _rope_slab(qn, cos, sin)
            q_ref[0, rows, j * LANES:(j + 1) * LANES] = qn.astype(BF16)

        kn = _head_rms_slab(z[:, O_K:O_K + KV_W], kg_ref[...])
        if rope:
            kn = _rope_slab(kn, cos, sin)
        _store_kv(kn, z[:, O_V:O_V + KV_W], k_ref, vt_ref, rows)

        g = _gelu_tanh(z[:, O_G:O_P])
        u_ref[0, rows, :] = g[:, :SGU_W]
        vn_ref[0, rows, :] = _rms(g[:, SGU_W:], sg_ref[...]).astype(BF16)

        p_ref[0, rows, :] = z[:, O_P:]

    starts = list(range(0, tm, sub))
    z = project(starts[0])
    for r0, r1 in zip(starts, starts[1:] + [None]):
        z_next = project(r1) if r1 is not None else None
        finish(r0, z)
        z = z_next


def _layer_spec(layer, *shape, **kwargs):
    return pl.BlockSpec((None, *shape), lambda *_: (layer,) + (0,) * len(shape), **kwargs)


def _inproj(layer, x, mod, mod_row_of_batch, norm1, w_in, qg, kg, sg, cos, sin, *, rope):
    B, N, _ = x.shape
    tm = min(INPROJ_TILE, N)
    grid = (B, N // tm)
    return pl.pallas_call(
        functools.partial(_inproj_kernel, rope=rope),
        grid=grid,
        in_specs=[
            pl.BlockSpec((1, tm, D_MODEL), lambda b, i: (b, i, 0)),
            pl.BlockSpec((None, 1, 6, D_MODEL), lambda b, i: (layer, mod_row_of_batch(b), 0, 0)),
            _layer_spec(layer, 1, D_MODEL),
            _layer_spec(layer, D_MODEL, IN_W),
            _layer_spec(layer, 1, LANES),
            _layer_spec(layer, 1, LANES),
            _layer_spec(layer, 1, SGU_W),
            pl.BlockSpec((tm, LANES), lambda b, i: (i, 0)),
            pl.BlockSpec((tm, LANES), lambda b, i: (i, 0)),
        ],
        out_specs=[
            pl.BlockSpec((1, tm, ATTN_W), lambda b, i: (b, i, 0)),
            pl.BlockSpec((1, N_KV_HEADS, tm, KV_W), lambda b, i: (b, 0, i, 0)),
            pl.BlockSpec((1, N_KV_HEADS, 2 * HEAD_DIM, tm), lambda b, i: (b, 0, 0, i)),
            pl.BlockSpec((1, tm, SGU_W), lambda b, i: (b, i, 0)),
            pl.BlockSpec((1, tm, SGU_W), lambda b, i: (b, i, 0)),
            pl.BlockSpec((1, tm, POOL_W), lambda b, i: (b, i, 0)),
        ],
        out_shape=[
            jax.ShapeDtypeStruct((B, N, ATTN_W), BF16),
            jax.ShapeDtypeStruct((B, N_KV_HEADS, N, KV_W), BF16),
            jax.ShapeDtypeStruct((B, N_KV_HEADS, 2 * HEAD_DIM, N), BF16),
            jax.ShapeDtypeStruct((B, N, SGU_W), F32),
            jax.ShapeDtypeStruct((B, N, SGU_W), BF16),
            jax.ShapeDtypeStruct((B, N, POOL_W), F32),
        ],
        compiler_params=pltpu.CompilerParams(
            dimension_semantics=("parallel", "parallel"), vmem_limit_bytes=VMEM_LIMIT),
        name="inproj_rope" if rope else "inproj",
    )(x, mod, norm1, w_in, qg, kg, sg, cos, sin)


def _attn_kernel(*refs, n_kv_sets, n_cast):
    bound_ref, q_ref = refs[:2]
    refs = refs[1:]
    k_refs = refs[1:1 + n_kv_sets]
    vt_refs = refs[1 + n_kv_sets:1 + 2 * n_kv_sets]
    n_in = 1 + 2 * n_kv_sets + n_cast
    o_ref = refs[n_in]
    for src, dst in zip(refs[n_in - n_cast:n_in], refs[n_in + 1:]):
        dst[...] = src[...].astype(BF16)
    tq = q_ref.shape[1]
    lo = _half_mask((tq, LANES))
    zero = jnp.zeros((tq, LANES), BF16)
    nt = (((1,), (1,)), ((), ()))

    def attend(bound):
        def scores(idx):
            slab, half, kvh = idx // 2, idx % 2, idx // Q_PER_KV
            qs = q_ref[0, :, slab * LANES:(slab + 1) * LANES]
            qm = jnp.where(lo, qs, zero) if half == 0 else jnp.where(lo, zero, qs)
            st = [lax.dot_general(k[0, kvh], qm, nt, preferred_element_type=F32) for k in k_refs]
            if bound is not None:
                return st, bound
            return st, functools.reduce(
                jnp.maximum, [jnp.max(si, axis=0, keepdims=True) for si in st])

        def weighted_values(idx, st, shift):
            kvh = idx // Q_PER_KV
            acc = functools.reduce(lambda a, b: a + b, [
                jnp.dot(vt[0, kvh], jnp.exp2(si - shift).astype(BF16), preferred_element_type=F32)
                for si, vt in zip(st, vt_refs)])
            return acc[:HEAD_DIM] / acc[HEAD_DIM:HEAD_DIM + 1]

        heads = []
        pending = scores(0)
        for idx in range(N_HEADS):
            nxt = scores(idx + 1) if idx + 1 < N_HEADS else None
            heads.append(weighted_values(idx, *pending))
            pending = nxt
        o_ref[0] = jnp.concatenate(heads, axis=0).T.astype(BF16)

    bound = bound_ref[0, 0]
    safe = bound <= MAX_SCORE_BOUND

    @pl.when(safe)
    def _():
        attend(bound)

    @pl.when(jnp.logical_not(safe))
    def _():
        attend(None)


def _attention(score_bound, q, kv_sets, layer=None, cast_weights=()):
    B, N, _ = q.shape
    tq = min(Q_TILE, N)
    nq = N // tq
    n_steps = B * nq
    in_specs = [pl.BlockSpec(memory_space=pltpu.SMEM),
                pl.BlockSpec((1, tq, ATTN_W), lambda b, i: (b, i, 0))]
    for k, _ in kv_sets:
        in_specs.append(pl.BlockSpec((1, N_KV_HEADS, k.shape[2], KV_W), lambda b, i: (b, 0, 0, 0)))
    for _, vt in kv_sets:
        in_specs.append(pl.BlockSpec((1, N_KV_HEADS, 2 * HEAD_DIM, vt.shape[3]),
                                     lambda b, i: (b, 0, 0, 0)))
    out_specs = [pl.BlockSpec((1, tq, ATTN_W), lambda b, i: (b, i, 0))]
    out_shape = [jax.ShapeDtypeStruct((B, N, ATTN_W), BF16)]
    for w in cast_weights:
        _, rows, cols = w.shape
        per = next(d for d in (1, 2, 4, 8) if rows % (n_steps // d) == 0
                   and (rows // (n_steps // d)) % BF16_ROWS == 0)
        blk = rows // (n_steps // per)
        in_specs.append(pl.BlockSpec((None, blk, cols),
                                     lambda b, i, per=per: (layer, (b * nq + i) // per, 0)))
        out_specs.append(pl.BlockSpec((blk, cols), lambda b, i, per=per: ((b * nq + i) // per, 0)))
        out_shape.append(jax.ShapeDtypeStruct((rows, cols), BF16))
    outs = pl.pallas_call(
        functools.partial(_attn_kernel, n_kv_sets=len(kv_sets), n_cast=len(cast_weights)),
        grid=(B, nq),
        in_specs=in_specs,
        out_specs=out_specs,
        out_shape=out_shape,
        compiler_params=pltpu.CompilerParams(
            dimension_semantics=("arbitrary", "arbitrary"), vmem_limit_bytes=VMEM_LIMIT),
        name=f"attention_{len(kv_sets)}",
    )(score_bound, q, *[k for k, _ in kv_sets], *[vt for _, vt in kv_sets], *cast_weights)
    return outs if cast_weights else outs[0]


def _mix_stage(i, n_tiles, x_ref, mod_ref, attn_ref, u_ref, vn_ref, p_ref, pprev_ref, pnext_ref,
               ws_ref, bs_ref, pw_ref, ps_ref, wo_ref, n2_ref, seq_len):
    tm = x_ref.shape[1]
    m = mod_ref[0]

    lo = _half_mask((CHUNK, LANES))
    sgu_chunks = []
    for c in range(tm // CHUNK):
        slabs = []
        for j in range(SGU_W // LANES):
            vs = vn_ref[0, c * CHUNK:(c + 1) * CHUNK, j * LANES:(j + 1) * LANES]
            ra = jnp.dot(ws_ref[2 * j], vs, preferred_element_type=F32)
            rb = jnp.dot(ws_ref[2 * j + 1], vs, preferred_element_type=F32)
            slabs.append(jnp.where(lo, ra, rb))
        mixed = jnp.concatenate(slabs, axis=-1) + bs_ref[...]
        sgu_chunks.append((u_ref[0, c * CHUNK:(c + 1) * CHUNK, :] * mixed).astype(BF16))
    sgu = jnp.concatenate(sgu_chunks, axis=0)

    p = p_ref[0]
    prev = jnp.where(i > 0, pprev_ref[0], 0.0)
    nxt = jnp.where(i < n_tiles - 1, pnext_ref[0], 0.0)
    e = jnp.concatenate([prev, p, nxt], axis=0)
    n_ext = tm + 2 * POOL_HALO

    def ahead(a, k):
        return pltpu.roll(a, n_ext - k, axis=0)

    a2 = e + ahead(e, 1)
    a4 = a2 + ahead(a2, 2)
    a8 = a4 + ahead(a4, 4)
    a16 = a8 + ahead(a8, 8)
    s2 = ahead(a2, POOL_HALO - 1)[:tm]
    s4 = ahead(a4, POOL_HALO - 2)[:tm]
    s8 = ahead(a8, POOL_HALO - 4)[:tm]
    s16 = a16[:tm]
    grp = lax.broadcasted_iota(jnp.int32, (tm, POOL_W), 1) >> 6
    win_sum = jnp.where(grp == 0, s2, jnp.where(grp == 1, s4, jnp.where(grp == 2, s8, s16)))
    t = i * tm + lax.broadcasted_iota(jnp.int32, (tm, POOL_W), 0)
    half = jnp.left_shift(1, grp)
    cnt = jnp.minimum(t + half, seq_len) - jnp.maximum(t - half, 0)
    d = win_sum / cnt.astype(F32) - p
    pool = jnp.dot(d.astype(BF16), pw_ref[...], preferred_element_type=F32) * ps_ref[...]


    cat = jnp.concatenate([attn_ref[0], sgu, pool.astype(BF16)], axis=-1)
    mix = jnp.dot(cat, wo_ref[...], preferred_element_type=F32)
    x1 = x_ref[0] + m[2:3] * mix
    h2 = (_rms(x1, n2_ref[...]) * (1.0 + m[4:5]) + m[3:4]).astype(BF16)
    return x1, h2


def _ffn_stage(x1, h2, gate2, wg_ref, wu_ref, wd_ref, fn_ref, final):
    gate = jnp.dot(h2, wg_ref[...], preferred_element_type=F32)
    up = jnp.dot(h2, wu_ref[...], preferred_element_type=F32)
    act = (_silu(gate) * up).astype(BF16)
    x2 = x1 + gate2 * jnp.dot(act, wd_ref[...], preferred_element_type=F32)
    if final:
        x2 = _rms(x2, fn_ref[...])
    return x2


def _mixffn_kernel(x_ref, mod_ref, modp_ref, attn_ref, u_ref, vn_ref, p_ref, pprev_ref, pnext_ref,
                   ws_ref, bs_ref, pw_ref, ps_ref, wo_ref, n2_ref, wg_ref, wu_ref, wd_ref,
                   fn_ref, o_ref, x1_scr, h2_scr, *, seq_len, tiles_per_seq, final):
    s = pl.program_id(0)
    n_steps = pl.num_programs(0)
    i = lax.rem(jnp.minimum(s, n_steps - 2), tiles_per_seq)
    mix_args = (i, tiles_per_seq, x_ref, mod_ref, attn_ref, u_ref, vn_ref, p_ref, pprev_ref,
                pnext_ref, ws_ref, bs_ref, pw_ref, ps_ref, wo_ref, n2_ref, seq_len)
    ffn_args = (wg_ref, wu_ref, wd_ref, fn_ref, final)

    @pl.when(s == 0)
    def _():
        x1, h2 = _mix_stage(*mix_args)
        x1_scr[...] = x1
        h2_scr[...] = h2

    @pl.when(jnp.logical_and(s > 0, s < n_steps - 1))
    def _():
        o_ref[0] = _ffn_stage(x1_scr[...], h2_scr[...], modp_ref[0][5:6], *ffn_args)
        x1, h2 = _mix_stage(*mix_args)
        x1_scr[...] = x1
        h2_scr[...] = h2

    @pl.when(s == n_steps - 1)
    def _():
        o_ref[0] = _ffn_stage(x1_scr[...], h2_scr[...], modp_ref[0][5:6], *ffn_args)


def _mixffn(layer, x, mod, mod_row_of_batch, attn, u, vn, p, ws, bs, pw, ps, wo, n2, wg, wu, wd,
            fn, *, final):
    B, N, _ = x.shape
    tm = min(TOKEN_TILE, N)
    tps = N // tm
    n_tiles = B * tps
    hb = tm // POOL_HALO
    n_hblocks = N // POOL_HALO

    def cur(s):
        t = jnp.minimum(s, n_tiles - 1)
        return t // tps, lax.rem(t, tps)

    def prv(s):
        t = jnp.maximum(s - 1, 0)
        return t // tps, lax.rem(t, tps)

    def tile(w):
        return pl.BlockSpec((1, tm, w), lambda s: (*cur(s), 0))

    def halo_prev(s):
        b, i = cur(s)
        return b, jnp.maximum(i * hb - 1, 0), 0

    def halo_next(s):
        b, i = cur(s)
        return b, jnp.minimum((i + 1) * hb, n_hblocks - 1), 0

    once = dict(pipeline_mode=pl.Buffered(1))
    return pl.pallas_call(
        functools.partial(_mixffn_kernel, seq_len=N, tiles_per_seq=tps, final=final),
        grid=(n_tiles + 1,),
        in_specs=[
            tile(D_MODEL),
            pl.BlockSpec((None, 1, 6, D_MODEL),
                         lambda s: (layer, mod_row_of_batch(cur(s)[0]), 0, 0)),
            pl.BlockSpec((None, 1, 6, D_MODEL),
                         lambda s: (layer, mod_row_of_batch(prv(s)[0]), 0, 0)),
            tile(ATTN_W), tile(SGU_W), tile(SGU_W), tile(POOL_W),
            pl.BlockSpec((1, POOL_HALO, POOL_W), halo_prev),
            pl.BlockSpec((1, POOL_HALO, POOL_W), halo_next),
            _layer_spec(layer, SGU_GROUPS, CHUNK, CHUNK, **once),
            _layer_spec(layer, CHUNK, SGU_W, **once),
            _layer_spec(layer, POOL_W, POOL_W, **once),
            _layer_spec(layer, 1, POOL_W, **once),
            pl.BlockSpec((MIX_W, D_MODEL), lambda s: (0, 0), **once),
            _layer_spec(layer, 1, D_MODEL, **once),
            pl.BlockSpec((D_MODEL, D_FF), lambda s: (0, 0), **once),
            pl.BlockSpec((D_MODEL, D_FF), lambda s: (0, 0), **once),
            pl.BlockSpec((D_FF, D_MODEL), lambda s: (0, 0), **once),
            pl.BlockSpec((1, D_MODEL), lambda s: (0, 0), **once),
        ],
        out_specs=pl.BlockSpec((1, tm, D_MODEL), lambda s: (*prv(s), 0)),
        out_shape=jax.ShapeDtypeStruct((B, N, D_MODEL), F32),
        scratch_shapes=[pltpu.VMEM((tm, D_MODEL), F32), pltpu.VMEM((tm, D_MODEL), BF16)],
        compiler_params=pltpu.CompilerParams(
            dimension_semantics=("arbitrary",), vmem_limit_bytes=VMEM_LIMIT),
        name="mixffn_final" if final else "mixffn",
    )(x, mod, mod, attn, u, vn, p, p, p, ws, bs, pw, ps, wo, n2, wg, wu, wd, fn)


def _rope_tables(n_tokens):
    rows = jnp.repeat(jnp.arange(n_tokens // GRID_W, dtype=F32), GRID_W)
    cols = jnp.tile(jnp.arange(GRID_W, dtype=F32), n_tokens // GRID_W)
    inv = ROPE_THETA ** (-jnp.arange(ROPE_PAIRS_PER_AXIS, dtype=F32) / ROPE_PAIRS_PER_AXIS)
    ang = jnp.concatenate([rows[:, None] * inv, cols[:, None] * inv], axis=-1)
    cos, sin = jnp.cos(ang), jnp.sin(ang)
    return jnp.tile(cos, (1, 4)), jnp.tile(jnp.concatenate([-sin, sin], axis=-1), (1, 2))


def _block_diag(w):
    d, g, a, b = w.shape
    eye = jnp.eye(g, dtype=w.dtype)
    return (w[:, :, :, None, :] * eye[None, :, None, :, None]).reshape(d, g * a, g * b)


def kernel(x, c, ctx, c_ctx, w_mod, b_mod, norm1, norm2, w_in, q_norm, k_norm, sgu_norm, w_s, b_s,
           pool_w, pool_scale, w_out, w_gate, w_up, w_down, final_norm):
    B, S, _ = x.shape
    L = ctx.shape[1]
    c_rows = jnp.concatenate(
        [c, c_ctx[None, :], jnp.zeros((MOD_ROWS - B - 1, D_MODEL), F32)], axis=0)
    mod, w_in_bf = _modulation(c_rows, w_mod, b_mod, w_in)
    mod = mod.reshape(DEPTH, MOD_ROWS, 6, D_MODEL)
    cos, sin = _rope_tables(S)
    ones = jnp.ones((L, LANES), F32)

    row = lambda a: a[:, None, :]
    proj_w = (
        row(norm1),
        w_in_bf,
        row(jnp.tile(q_norm, (1, 2))),
        row(jnp.tile(k_norm, (1, 2))),
        row(sgu_norm),
    )
    ws_bf = w_s.astype(BF16)
    sgu_bias = jnp.repeat(jnp.swapaxes(b_s, 1, 2), SGU_W // SGU_GROUPS, axis=2)
    pool_bd = _block_diag(pool_w).astype(BF16)
    lat_row = lambda b: b
    ctx_row = lambda b: B

    for l in range(DEPTH):
        last = l == DEPTH - 1
        q, k, vt, u, vn, p = _inproj(l, x, mod, lat_row, *proj_w, cos, sin, rope=True)
        if last:
            kc, vtc = _kvproj(l, ctx, mod, ctx_row, proj_w[0], proj_w[1], proj_w[3])
        else:
            qc, kc, vtc, uc, vnc, pc = _inproj(l, ctx, mod, ctx_row, *proj_w, ones, ones,
                                               rope=False)
        bound = (HEAD_DIM * ATTN_SCALE * LOG2E * BOUND_MARGIN
                 * jnp.max(jnp.abs(q_norm[l])) * jnp.max(jnp.abs(k_norm[l]))).reshape(1, 1)
        attn, wo_bf, wg_bf, wu_bf, wd_bf = _attention(
            bound, q, [(kc, vtc), (k, vt)], layer=l, cast_weights=(w_out, w_gate, w_up, w_down))
        mix_w = (ws_bf, sgu_bias, pool_bd, row(pool_scale), wo_bf, row(norm2), wg_bf, wu_bf, wd_bf,
                 final_norm[None, :])
        if not last:
            attn_c = _attention(bound, qc, [(kc, vtc)])
            ctx = _mixffn(l, ctx, mod, ctx_row, attn_c, uc, vnc, pc, *mix_w, final=False)
        x = _mixffn(l, x, mod, lat_row, attn, u, vn, p, *mix_w, final=last)
    return x
```

```python
import functools
import math

import jax
import jax.numpy as jnp
import numpy as np
from jax import lax
from jax.experimental import pallas as pl
from jax.experimental.pallas import tpu as pltpu

F32 = jnp.float32
BF16 = jnp.bfloat16

D_MODEL = 1024
DEPTH = 2
GRID_W = 64
EPS = 1e-6

HEAD_DIM = 64
N_HEADS = 8
N_KV_HEADS = 2
Q_PER_KV = N_HEADS // N_KV_HEADS
ATTN_W = N_HEADS * HEAD_DIM
KV_W = N_KV_HEADS * HEAD_DIM
ROPE_THETA = 10000.0
ROPE_PAIRS_PER_AXIS = HEAD_DIM // 4
ATTN_SCALE = HEAD_DIM ** -0.5
LOG2E = math.log2(math.e)
BOUND_MARGIN = 1.02
MAX_SCORE_BOUND = 50.0

CHUNK = 128
SGU_W = D_MODEL // 4
SGU_GROUPS = 4
POOL_WINDOWS = (2, 4, 8, 16)
POOL_W = D_MODEL // 4
POOL_HALO = 8

MIX_W = ATTN_W + SGU_W + POOL_W
O_Q = 0
O_K = O_Q + ATTN_W
O_V = O_K + KV_W
O_G = O_V + KV_W
O_P = O_G + 2 * SGU_W
IN_W = O_P + POOL_W
D_FF = 2816

LANES = 128
BF16_ROWS = 16
MOD_ROWS = 16
VMEM_LIMIT = 56 * 1024 * 1024

TOKEN_TILE = 512
INPROJ_TILE = 1024
SUB_TILE = 256
Q_TILE = 1024


def _half_mask(shape):
    lane = lax.broadcasted_iota(jnp.int32, shape, len(shape) - 1)
    return (lane & HEAD_DIM) == 0


def _rms(x, gain):
    return x * lax.rsqrt(jnp.mean(x * x, axis=-1, keepdims=True) + EPS) * gain


def _head_rms_slab(z, gain):
    lo = _half_mask(z.shape)
    sq = z * z
    s0 = jnp.sum(jnp.where(lo, sq, 0.0), axis=-1, keepdims=True)
    s1 = jnp.sum(jnp.where(lo, 0.0, sq), axis=-1, keepdims=True)
    r0 = lax.rsqrt(s0 * (1.0 / HEAD_DIM) + EPS)
    r1 = lax.rsqrt(s1 * (1.0 / HEAD_DIM) + EPS)
    return z * jnp.where(lo, r0, r1) * gain


def _rope_slab(x, cos, sin_signed):
    lane = lax.broadcasted_iota(jnp.int32, x.shape, 1)
    first = (lane & (HEAD_DIM // 2)) == 0
    up = pltpu.roll(x, LANES - HEAD_DIM // 2, axis=1)
    down = pltpu.roll(x, HEAD_DIM // 2, axis=1)
    partner = jnp.where(first, up, down)
    return x * cos + partner * sin_signed


def _gelu_tanh(x):
    c = math.sqrt(2.0 / math.pi)
    k0 = -2.0 * c * LOG2E
    k1 = k0 * 0.044715
    return x / (1.0 + jnp.exp2(x * (k1 * (x * x) + k0)))


def _silu(x):
    return x / (1.0 + jnp.exp(-x))


def _mod_kernel(c_ref, w_ref, b_ref, win_ref, o_ref, win_bf_ref):
    win_bf_ref[...] = win_ref[...].astype(BF16)

    a = _silu(c_ref[...])
    w = w_ref[0]
    a_hi = a.astype(BF16)
    a_lo = (a - a_hi.astype(F32)).astype(BF16)
    w_hi = w.astype(BF16)
    w_lo = (w - w_hi.astype(F32)).astype(BF16)
    dot = functools.partial(jnp.dot, preferred_element_type=F32)
    o_ref[0] = dot(a_hi, w_hi) + (dot(a_lo, w_hi) + dot(a_hi, w_lo)) + b_ref[0]


def _modulation(c_rows, w_mod, b_mod, w_in):
    n_col = 6
    win_cols = IN_W // n_col
    win_spec = pl.BlockSpec((None, D_MODEL, win_cols), lambda l, j: (l, 0, j))
    return pl.pallas_call(
        _mod_kernel,
        grid=(DEPTH, n_col),
        in_specs=[
            pl.BlockSpec((MOD_ROWS, D_MODEL), lambda l, j: (0, 0)),
            pl.BlockSpec((1, D_MODEL, D_MODEL), lambda l, j: (l, 0, j)),
            pl.BlockSpec((1, 1, D_MODEL), lambda l, j: (l, 0, j)),
            win_spec,
        ],
        out_specs=[pl.BlockSpec((1, MOD_ROWS, D_MODEL), lambda l, j: (l, 0, j)), win_spec],
        out_shape=[jax.ShapeDtypeStruct((DEPTH, MOD_ROWS, 6 * D_MODEL), F32),
                   jax.ShapeDtypeStruct(w_in.shape, BF16)],
        compiler_params=pltpu.CompilerParams(dimension_semantics=("parallel", "parallel")),
        name="modulation",
    )(c_rows, w_mod, b_mod.reshape(DEPTH, 1, 6 * D_MODEL), w_in)


def _store_kv(kn, v, k_ref, vt_ref, rows):
    k_swapped = pltpu.roll(kn, HEAD_DIM, axis=1)
    lo = _half_mask(kn.shape)
    k_ref[0, 0, rows, :] = jnp.where(lo, kn, k_swapped).astype(BF16)
    k_ref[0, 1, rows, :] = jnp.where(lo, k_swapped, kn).astype(BF16)
    vt = v.T
    ones = jnp.ones((HEAD_DIM, vt.shape[1]), F32)
    for hh in range(N_KV_HEADS):
        vt_ref[0, hh, :, rows] = jnp.concatenate(
            [vt[hh * HEAD_DIM:(hh + 1) * HEAD_DIM], ones], axis=0).astype(BF16)


def _kvproj_kernel(x_ref, mod_ref, n1_ref, w_ref, kg_ref, k_ref, vt_ref):
    m = mod_ref[0]
    h = _rms(x_ref[0], n1_ref[...] * (1.0 + m[1:2])) + m[0:1]
    z = jnp.dot(h.astype(BF16), w_ref[...], preferred_element_type=F32)
    kn = _head_rms_slab(z[:, :KV_W], kg_ref[...])
    _store_kv(kn, z[:, KV_W:], k_ref, vt_ref, slice(None))


def _kvproj(layer, x, mod, mod_row_of_batch, norm1, w_in, kg):
    B, N, _ = x.shape
    tm = min(INPROJ_TILE, N)
    assert O_K % (2 * KV_W) == 0 and O_V == O_K + KV_W
    return pl.pallas_call(
        _kvproj_kernel,
        grid=(B, N // tm),
        in_specs=[
            pl.BlockSpec((1, tm, D_MODEL), lambda b, i: (b, i, 0)),
            pl.BlockSpec((None, 1, 6, D_MODEL), lambda b, i: (layer, mod_row_of_batch(b), 0, 0)),
            _layer_spec(layer, 1, D_MODEL),
            pl.BlockSpec((None, D_MODEL, 2 * KV_W), lambda b, i: (layer, 0, O_K // (2 * KV_W))),
            _layer_spec(layer, 1, LANES),
        ],
        out_specs=[
            pl.BlockSpec((1, N_KV_HEADS, tm, KV_W), lambda b, i: (b, 0, i, 0)),
            pl.BlockSpec((1, N_KV_HEADS, 2 * HEAD_DIM, tm), lambda b, i: (b, 0, 0, i)),
        ],
        out_shape=[
            jax.ShapeDtypeStruct((B, N_KV_HEADS, N, KV_W), BF16),
            jax.ShapeDtypeStruct((B, N_KV_HEADS, 2 * HEAD_DIM, N), BF16),
        ],
        compiler_params=pltpu.CompilerParams(
            dimension_semantics=("parallel", "parallel"), vmem_limit_bytes=VMEM_LIMIT),
        name="kvproj",
    )(x, mod, norm1, w_in, kg)

def _inproj_kernel(x_ref, mod_ref, n1_ref, w_ref, qg_ref, kg_ref, sg_ref, cos_ref, sin_ref,
                   q_ref, k_ref, vt_ref, u_ref, vn_ref, p_ref, *, rope):
    m = mod_ref[0]
    tm = x_ref.shape[1]
    sub = min(SUB_TILE, tm)

    gain1 = n1_ref[...] * (1.0 + m[1:2])
    q_gain = qg_ref[...] * (ATTN_SCALE * LOG2E)

    def normalise(r0):
        return (_rms(x_ref[0, r0:r0 + sub, :], gain1) + m[0:1]).astype(BF16)

    def project(h):
        return jnp.dot(h, w_ref[...], preferred_element_type=F32)

    def finish(r0, z):
        rows = slice(r0, r0 + sub)
        if rope:
            cos = cos_ref[rows, :]
            sin = sin_ref[rows, :]

        for j in range(ATTN_W // LANES):
            zq = z[:, O_Q + j * LANES:O_Q + (j + 1) * LANES]
            qn = _head_rms_slab(zq, q_gain)
            if rope:
                qn = _rope_slab(qn, cos, sin)
            q_ref[0, rows, j * LANES:(j + 1) * LANES] = qn.astype(BF16)

        kn = _head_rms_slab(z[:, O_K:O_K + KV_W], kg_ref[...])
        if rope:
            kn = _rope_slab(kn, cos, sin)
        _store_kv(kn, z[:, O_V:O_V + KV_W], k_ref, vt_ref, rows)

        g = _gelu_tanh(z[:, O_G:O_P])
        u_ref[0, rows, :] = g[:, :SGU_W]
        vn_ref[0, rows, :] = _rms(g[:, SGU_W:], sg_ref[...]).astype(BF16)

        p_ref[0, rows, :] = z[:, O_P:]

    starts = list(range(0, tm, sub))
    z = project(normalise(starts[0]))
    for r0, r1 in zip(starts, starts[1:] + [None]):
        z_next = project(normalise(r1)) if r1 is not None else None
        finish(r0, z)
        z = z_next


def _layer_spec(layer, *shape, **kwargs):
    return pl.BlockSpec((None, *shape), lambda *_: (layer,) + (0,) * len(shape), **kwargs)


def _inproj(layer, x, mod, mod_row_of_batch, norm1, w_in, qg, kg, sg, cos, sin, *, rope):
    B, N, _ = x.shape
    tm = min(INPROJ_TILE, N)
    grid = (B, N // tm)
    return pl.pallas_call(
        functools.partial(_inproj_kernel, rope=rope),
        grid=grid,
        in_specs=[
            pl.BlockSpec((1, tm, D_MODEL), lambda b, i: (b, i, 0)),
            pl.BlockSpec((None, 1, 6, D_MODEL), lambda b, i: (layer, mod_row_of_batch(b), 0, 0)),
            _layer_spec(layer, 1, D_MODEL),
            _layer_spec(layer, D_MODEL, IN_W),
            _layer_spec(layer, 1, LANES),
            _layer_spec(layer, 1, LANES),
            _layer_spec(layer, 1, SGU_W),
            pl.BlockSpec((tm, LANES), lambda b, i: (i, 0)),
            pl.BlockSpec((tm, LANES), lambda b, i: (i, 0)),
        ],
        out_specs=[
            pl.BlockSpec((1, tm, ATTN_W), lambda b, i: (b, i, 0)),
            pl.BlockSpec((1, N_KV_HEADS, tm, KV_W), lambda b, i: (b, 0, i, 0)),
            pl.BlockSpec((1, N_KV_HEADS, 2 * HEAD_DIM, tm), lambda b, i: (b, 0, 0, i)),
            pl.BlockSpec((1, tm, SGU_W), lambda b, i: (b, i, 0)),
            pl.BlockSpec((1, tm, SGU_W), lambda b, i: (b, i, 0)),
            pl.BlockSpec((1, tm, POOL_W), lambda b, i: (b, i, 0)),
        ],
        out_shape=[
            jax.ShapeDtypeStruct((B, N, ATTN_W), BF16),
            jax.ShapeDtypeStruct((B, N_KV_HEADS, N, KV_W), BF16),
            jax.ShapeDtypeStruct((B, N_KV_HEADS, 2 * HEAD_DIM, N), BF16),
            jax.ShapeDtypeStruct((B, N, SGU_W), F32),
            jax.ShapeDtypeStruct((B, N, SGU_W), BF16),
            jax.ShapeDtypeStruct((B, N, POOL_W), F32),
        ],
        compiler_params=pltpu.CompilerParams(
            dimension_semantics=("parallel", "parallel"), vmem_limit_bytes=VMEM_LIMIT),
        name="inproj_rope" if rope else "inproj",
    )(x, mod, norm1, w_in, qg, kg, sg, cos, sin)


def _attn_kernel(*refs, layer, n_kv_sets, n_cast):
    bound_ref, q_ref = refs[:2]
    refs = refs[1:]
    k_refs = refs[1:1 + n_kv_sets]
    vt_refs = refs[1 + n_kv_sets:1 + 2 * n_kv_sets]
    n_in = 1 + 2 * n_kv_sets + n_cast
    o_ref = refs[n_in]
    for src, dst in zip(refs[n_in - n_cast:n_in], refs[n_in + 1:]):
        dst[...] = src[...].astype(BF16)
    tq = q_ref.shape[1]
    lo = _half_mask((tq, LANES))
    zero = jnp.zeros((tq, LANES), BF16)
    nt = (((1,), (1,)), ((), ()))

    def attend(bound):
        def scores(idx):
            slab, half, kvh = idx // 2, idx % 2, idx // Q_PER_KV
            qs = q_ref[0, :, slab * LANES:(slab + 1) * LANES]
            qm = jnp.where(lo, qs, zero) if half == 0 else jnp.where(lo, zero, qs)
            st = [lax.dot_general(k[0, kvh], qm, nt, preferred_element_type=F32) for k in k_refs]
            if bound is not None:
                return st, bound
            return st, functools.reduce(
                jnp.maximum, [jnp.max(si, axis=0, keepdims=True) for si in st])

        def weighted_values(idx, st, shift):
            kvh = idx // Q_PER_KV
            acc = functools.reduce(lambda a, b: a + b, [
                jnp.dot(vt[0, kvh], jnp.exp2(si - shift).astype(BF16), preferred_element_type=F32)
                for si, vt in zip(st, vt_refs)])
            return acc[:HEAD_DIM] / acc[HEAD_DIM:HEAD_DIM + 1]

        heads = []
        pending = scores(0)
        for idx in range(N_HEADS):
            nxt = scores(idx + 1) if idx + 1 < N_HEADS else None
            heads.append(weighted_values(idx, *pending))
            pending = nxt
        o_ref[0] = jnp.concatenate(heads, axis=0).T.astype(BF16)

    bound = bound_ref[layer, 0]
    safe = bound <= MAX_SCORE_BOUND

    @pl.when(safe)
    def _():
        attend(bound)

    @pl.when(jnp.logical_not(safe))
    def _():
        attend(None)


def _attention(layer, score_bounds, q, kv_sets, cast_weights=()):
    B, N, _ = q.shape
    tq = min(Q_TILE, N)
    nq = N // tq
    n_steps = B * nq
    in_specs = [pl.BlockSpec(memory_space=pltpu.SMEM),
                pl.BlockSpec((1, tq, ATTN_W), lambda b, i: (b, i, 0))]
    for k, _ in kv_sets:
        in_specs.append(pl.BlockSpec((1, N_KV_HEADS, k.shape[2], KV_W), lambda b, i: (b, 0, 0, 0)))
    for _, vt in kv_sets:
        in_specs.append(pl.BlockSpec((1, N_KV_HEADS, 2 * HEAD_DIM, vt.shape[3]),
                                     lambda b, i: (b, 0, 0, 0)))
    out_specs = [pl.BlockSpec((1, tq, ATTN_W), lambda b, i: (b, i, 0))]
    out_shape = [jax.ShapeDtypeStruct((B, N, ATTN_W), BF16)]
    for w in cast_weights:
        _, rows, cols = w.shape
        per = next(d for d in (1, 2, 4, 8) if rows % (n_steps // d) == 0
                   and (rows // (n_steps // d)) % BF16_ROWS == 0)
        blk = rows // (n_steps // per)
        in_specs.append(pl.BlockSpec((None, blk, cols),
                                     lambda b, i, per=per: (layer, (b * nq + i) // per, 0)))
        out_specs.append(pl.BlockSpec((blk, cols), lambda b, i, per=per: ((b * nq + i) // per, 0)))
        out_shape.append(jax.ShapeDtypeStruct((rows, cols), BF16))
    outs = pl.pallas_call(
        functools.partial(_attn_kernel, layer=layer, n_kv_sets=len(kv_sets),
                          n_cast=len(cast_weights)),
        grid=(B, nq),
        in_specs=in_specs,
        out_specs=out_specs,
        out_shape=out_shape,
        compiler_params=pltpu.CompilerParams(
            dimension_semantics=("arbitrary", "arbitrary"), vmem_limit_bytes=VMEM_LIMIT),
        name=f"attention_{len(kv_sets)}",
    )(score_bounds, q, *[k for k, _ in kv_sets], *[vt for _, vt in kv_sets], *cast_weights)
    return outs if cast_weights else outs[0]


def _mix_stage(i, n_tiles, x_ref, mod_ref, attn_ref, u_ref, vn_ref, p_ref, pprev_ref, pnext_ref,
               ws_ref, bs_ref, pw_ref, ps_ref, wo_ref, n2_ref, seq_len):
    tm = x_ref.shape[1]
    m = mod_ref[0]

    lo = _half_mask((CHUNK, LANES))
    sgu_chunks = []
    for c in range(tm // CHUNK):
        slabs = []
        for j in range(SGU_W // LANES):
            vs = vn_ref[0, c * CHUNK:(c + 1) * CHUNK, j * LANES:(j + 1) * LANES]
            ra = jnp.dot(ws_ref[2 * j], vs, preferred_element_type=F32)
            rb = jnp.dot(ws_ref[2 * j + 1], vs, preferred_element_type=F32)
            slabs.append(jnp.where(lo, ra, rb))
        mixed = jnp.concatenate(slabs, axis=-1) + bs_ref[...]
        sgu_chunks.append((u_ref[0, c * CHUNK:(c + 1) * CHUNK, :] * mixed).astype(BF16))
    sgu = jnp.concatenate(sgu_chunks, axis=0)

    p = p_ref[0]
    prev = jnp.where(i > 0, pprev_ref[0], 0.0)
    nxt = jnp.where(i < n_tiles - 1, pnext_ref[0], 0.0)
    e = jnp.concatenate([prev, p, nxt], axis=0)
    n_ext = tm + 2 * POOL_HALO

    def ahead(a, k):
        return pltpu.roll(a, n_ext - k, axis=0)

    a2 = e + ahead(e, 1)
    a4 = a2 + ahead(a2, 2)
    a8 = a4 + ahead(a4, 4)
    a16 = a8 + ahead(a8, 8)
    s2 = ahead(a2, POOL_HALO - 1)[:tm]
    s4 = ahead(a4, POOL_HALO - 2)[:tm]
    s8 = ahead(a8, POOL_HALO - 4)[:tm]
    s16 = a16[:tm]
    grp = lax.broadcasted_iota(jnp.int32, (tm, POOL_W), 1) >> 6
    win_sum = jnp.where(grp == 0, s2, jnp.where(grp == 1, s4, jnp.where(grp == 2, s8, s16)))
    t = i * tm + lax.broadcasted_iota(jnp.int32, (tm, POOL_W), 0)
    half = jnp.left_shift(1, grp)
    cnt = jnp.minimum(t + half, seq_len) - jnp.maximum(t - half, 0)
    d = win_sum / cnt.astype(F32) - p
    pool = jnp.dot(d.astype(BF16), pw_ref[...], preferred_element_type=F32) * ps_ref[...]


    cat = jnp.concatenate([attn_ref[0], sgu, pool.astype(BF16)], axis=-1)
    mix = jnp.dot(cat, wo_ref[...], preferred_element_type=F32)
    x1 = x_ref[0] + m[2:3] * mix
    h2 = (_rms(x1, n2_ref[...]) * (1.0 + m[4:5]) + m[3:4]).astype(BF16)
    return x1, h2


def _ffn_stage(x1, h2, gate2, wg_ref, wu_ref, wd_ref, fn_ref, final):
    gate = jnp.dot(h2, wg_ref[...], preferred_element_type=F32)
    up = jnp.dot(h2, wu_ref[...], preferred_element_type=F32)
    act = (_silu(gate) * up).astype(BF16)
    x2 = x1 + gate2 * jnp.dot(act, wd_ref[...], preferred_element_type=F32)
    if final:
        x2 = _rms(x2, fn_ref[...])
    return x2


def _mixffn_kernel(x_ref, mod_ref, modp_ref, attn_ref, u_ref, vn_ref, p_ref, pprev_ref, pnext_ref,
                   ws_ref, bs_ref, pw_ref, ps_ref, wo_ref, n2_ref, wg_ref, wu_ref, wd_ref,
                   fn_ref, o_ref, x1_scr, h2_scr, *, seq_len, tiles_per_seq, final):
    s = pl.program_id(0)
    n_steps = pl.num_programs(0)
    i = lax.rem(jnp.minimum(s, n_steps - 2), tiles_per_seq)
    mix_args = (i, tiles_per_seq, x_ref, mod_ref, attn_ref, u_ref, vn_ref, p_ref, pprev_ref,
                pnext_ref, ws_ref, bs_ref, pw_ref, ps_ref, wo_ref, n2_ref, seq_len)
    ffn_args = (wg_ref, wu_ref, wd_ref, fn_ref, final)

    @pl.when(s == 0)
    def _():
        x1, h2 = _mix_stage(*mix_args)
        x1_scr[...] = x1
        h2_scr[...] = h2

    @pl.when(jnp.logical_and(s > 0, s < n_steps - 1))
    def _():
        o_ref[0] = _ffn_stage(x1_scr[...], h2_scr[...], modp_ref[0][5:6], *ffn_args)
        x1, h2 = _mix_stage(*mix_args)
        x1_scr[...] = x1
        h2_scr[...] = h2

    @pl.when(s == n_steps - 1)
    def _():
        o_ref[0] = _ffn_stage(x1_scr[...], h2_scr[...], modp_ref[0][5:6], *ffn_args)


def _mixffn(layer, x, mod, mod_row_of_batch, attn, u, vn, p, ws, bs, pw, ps, wo, n2, wg, wu, wd,
            fn, *, final):
    B, N, _ = x.shape
    tm = min(TOKEN_TILE, N)
    tps = N // tm
    n_tiles = B * tps
    hb = tm // POOL_HALO
    n_hblocks = N // POOL_HALO

    def cur(s):
        t = jnp.minimum(s, n_tiles - 1)
        return t // tps, lax.rem(t, tps)

    def prv(s):
        t = jnp.maximum(s - 1, 0)
        return t // tps, lax.rem(t, tps)

    def tile(w):
        return pl.BlockSpec((1, tm, w), lambda s: (*cur(s), 0))

    def halo_prev(s):
        b, i = cur(s)
        return b, jnp.maximum(i * hb - 1, 0), 0

    def halo_next(s):
        b, i = cur(s)
        return b, jnp.minimum((i + 1) * hb, n_hblocks - 1), 0

    once = dict(pipeline_mode=pl.Buffered(1))
    return pl.pallas_call(
        functools.partial(_mixffn_kernel, seq_len=N, tiles_per_seq=tps, final=final),
        grid=(n_tiles + 1,),
        in_specs=[
            tile(D_MODEL),
            pl.BlockSpec((None, 1, 6, D_MODEL),
                         lambda s: (layer, mod_row_of_batch(cur(s)[0]), 0, 0)),
            pl.BlockSpec((None, 1, 6, D_MODEL),
                         lambda s: (layer, mod_row_of_batch(prv(s)[0]), 0, 0)),
            tile(ATTN_W), tile(SGU_W), tile(SGU_W), tile(POOL_W),
            pl.BlockSpec((1, POOL_HALO, POOL_W), halo_prev),
            pl.BlockSpec((1, POOL_HALO, POOL_W), halo_next),
            _layer_spec(layer, SGU_GROUPS, CHUNK, CHUNK, **once),
            _layer_spec(layer, CHUNK, SGU_W, **once),
            _layer_spec(layer, POOL_W, POOL_W, **once),
            _layer_spec(layer, 1, POOL_W, **once),
            pl.BlockSpec((MIX_W, D_MODEL), lambda s: (0, 0), **once),
            _layer_spec(layer, 1, D_MODEL, **once),
            pl.BlockSpec((D_MODEL, D_FF), lambda s: (0, 0), **once),
            pl.BlockSpec((D_MODEL, D_FF), lambda s: (0, 0), **once),
            pl.BlockSpec((D_FF, D_MODEL), lambda s: (0, 0), **once),
            pl.BlockSpec((1, D_MODEL), lambda s: (0, 0), **once),
        ],
        out_specs=pl.BlockSpec((1, tm, D_MODEL), lambda s: (*prv(s), 0)),
        out_shape=jax.ShapeDtypeStruct((B, N, D_MODEL), F32),
        scratch_shapes=[pltpu.VMEM((tm, D_MODEL), F32), pltpu.VMEM((tm, D_MODEL), BF16)],
        compiler_params=pltpu.CompilerParams(
            dimension_semantics=("arbitrary",), vmem_limit_bytes=VMEM_LIMIT),
        name="mixffn_final" if final else "mixffn",
    )(x, mod, mod, attn, u, vn, p, p, p, ws, bs, pw, ps, wo, n2, wg, wu, wd, fn)


def _rope_tables(n_tokens):
    rows = np.repeat(np.arange(n_tokens // GRID_W, dtype=np.float64), GRID_W)
    cols = np.tile(np.arange(GRID_W, dtype=np.float64), n_tokens // GRID_W)
    inv = ROPE_THETA ** (-np.arange(ROPE_PAIRS_PER_AXIS, dtype=np.float64) / ROPE_PAIRS_PER_AXIS)
    ang = np.concatenate([rows[:, None] * inv, cols[:, None] * inv], axis=-1)
    cos, sin = np.cos(ang), np.sin(ang)
    return (np.tile(cos, (1, 4)).astype(np.float32),
            np.tile(np.concatenate([-sin, sin], axis=-1), (1, 2)).astype(np.float32))


def _block_diag(w):
    d, g, a, b = w.shape
    eye = jnp.eye(g, dtype=w.dtype)
    return (w[:, :, :, None, :] * eye[None, :, None, :, None]).reshape(d, g * a, g * b)


def kernel(x, c, ctx, c_ctx, w_mod, b_mod, norm1, norm2, w_in, q_norm, k_norm, sgu_norm, w_s, b_s,
           pool_w, pool_scale, w_out, w_gate, w_up, w_down, final_norm):
    B, S, _ = x.shape
    L = ctx.shape[1]
    c_rows = jnp.concatenate(
        [c, c_ctx[None, :], jnp.zeros((MOD_ROWS - B - 1, D_MODEL), F32)], axis=0)
    mod, w_in_bf = _modulation(c_rows, w_mod, b_mod, w_in)
    mod = mod.reshape(DEPTH, MOD_ROWS, 6, D_MODEL)
    cos, sin = _rope_tables(S)
    ones = np.ones((L, LANES), np.float32)

    row = lambda a: a[:, None, :]
    proj_w = (
        row(norm1),
        w_in_bf,
        row(jnp.tile(q_norm, (1, 2))),
        row(jnp.tile(k_norm, (1, 2))),
        row(sgu_norm),
    )
    ws_bf = w_s.astype(BF16)
    sgu_bias = jnp.repeat(jnp.swapaxes(b_s, 1, 2), SGU_W // SGU_GROUPS, axis=2)
    pool_bd = _block_diag(pool_w).astype(BF16)
    lat_row = lambda b: b
    ctx_row = lambda b: B
    bounds = (HEAD_DIM * ATTN_SCALE * LOG2E * BOUND_MARGIN
              * jnp.max(jnp.abs(q_norm), axis=1, keepdims=True)
              * jnp.max(jnp.abs(k_norm), axis=1, keepdims=True))

    for l in range(DEPTH):
        last = l == DEPTH - 1
        q, k, vt, u, vn, p = _inproj(l, x, mod, lat_row, *proj_w, cos, sin, rope=True)
        if last:
            kc, vtc = _kvproj(l, ctx, mod, ctx_row, proj_w[0], proj_w[1], proj_w[3])
        else:
            qc, kc, vtc, uc, vnc, pc = _inproj(l, ctx, mod, ctx_row, *proj_w, ones, ones,
                                               rope=False)
        attn, wo_bf, wg_bf, wu_bf, wd_bf = _attention(
            l, bounds, q, [(kc, vtc), (k, vt)], cast_weights=(w_out, w_gate, w_up, w_down))
        mix_w = (ws_bf, sgu_bias, pool_bd, row(pool_scale), wo_bf, row(norm2), wg_bf, wu_bf, wd_bf,
                 final_norm[None, :])
        if not last:
            attn_c = _attention(l, bounds, qc, [(kc, vtc)])
            ctx = _mixffn(l, ctx, mod, ctx_row, attn_c, uc, vnc, pc, *mix_w, final=False)
        x = _mixffn(l, x, mod, lat_row, attn, u, vn, p, *mix_w, final=last)
    return x
```

```python
import functools
import math

import jax
import jax.numpy as jnp
import numpy as np
from jax import lax
from jax.experimental import pallas as pl
from jax.experimental.pallas import tpu as pltpu

F32 = jnp.float32
BF16 = jnp.bfloat16

D_MODEL = 1024
DEPTH = 2
GRID_W = 64
EPS = 1e-6

HEAD_DIM = 64
N_HEADS = 8
N_KV_HEADS = 2
Q_PER_KV = N_HEADS // N_KV_HEADS
ATTN_W = N_HEADS * HEAD_DIM
KV_W = N_KV_HEADS * HEAD_DIM
ROPE_THETA = 10000.0
ROPE_PAIRS_PER_AXIS = HEAD_DIM // 4
ATTN_SCALE = HEAD_DIM ** -0.5
LOG2E = math.log2(math.e)
BOUND_MARGIN = 1.02
MAX_SCORE_BOUND = 50.0

CHUNK = 128
SGU_W = D_MODEL // 4
SGU_GROUPS = 4
POOL_WINDOWS = (2, 4, 8, 16)
POOL_W = D_MODEL // 4
POOL_HALO = 8

MIX_W = ATTN_W + SGU_W + POOL_W
O_Q = 0
O_K = O_Q + ATTN_W
O_V = O_K + KV_W
O_G = O_V + KV_W
O_P = O_G + 2 * SGU_W
IN_W = O_P + POOL_W
D_FF = 2816

LANES = 128
BF16_ROWS = 16
MOD_ROWS = 16
VMEM_LIMIT = 56 * 1024 * 1024

TOKEN_TILE = 512
INPROJ_TILE = 1024
SUB_TILE = 256
Q_TILE = 512


def _half_mask(shape):
    lane = lax.broadcasted_iota(jnp.int32, shape, len(shape) - 1)
    return (lane & HEAD_DIM) == 0


def _rms(x, gain):
    return x * lax.rsqrt(jnp.mean(x * x, axis=-1, keepdims=True) + EPS) * gain


def _head_rms_slab(z, gain):
    lo = _half_mask(z.shape)
    sq = z * z
    s0 = jnp.sum(jnp.where(lo, sq, 0.0), axis=-1, keepdims=True)
    s1 = jnp.sum(jnp.where(lo, 0.0, sq), axis=-1, keepdims=True)
    r0 = lax.rsqrt(s0 * (1.0 / HEAD_DIM) + EPS)
    r1 = lax.rsqrt(s1 * (1.0 / HEAD_DIM) + EPS)
    return z * jnp.where(lo, r0, r1) * gain


def _rope_slab(x, cos, sin_signed):
    lane = lax.broadcasted_iota(jnp.int32, x.shape, 1)
    first = (lane & (HEAD_DIM // 2)) == 0
    up = pltpu.roll(x, LANES - HEAD_DIM // 2, axis=1)
    down = pltpu.roll(x, HEAD_DIM // 2, axis=1)
    partner = jnp.where(first, up, down)
    return x * cos + partner * sin_signed


def _gelu_tanh(x):
    c = math.sqrt(2.0 / math.pi)
    k0 = -2.0 * c * LOG2E
    k1 = k0 * 0.044715
    return x / (1.0 + jnp.exp2(x * (k1 * (x * x) + k0)))


def _silu(x):
    return x / (1.0 + jnp.exp(-x))


def _mod_kernel(c_ref, w_ref, b_ref, win_ref, o_ref, win_bf_ref):
    win_bf_ref[...] = win_ref[...].astype(BF16)

    a = _silu(c_ref[...])
    w = w_ref[0]
    a_hi = a.astype(BF16)
    a_lo = (a - a_hi.astype(F32)).astype(BF16)
    w_hi = w.astype(BF16)
    w_lo = (w - w_hi.astype(F32)).astype(BF16)
    dot = functools.partial(jnp.dot, preferred_element_type=F32)
    o_ref[0] = dot(a_hi, w_hi) + (dot(a_lo, w_hi) + dot(a_hi, w_lo)) + b_ref[0]


def _modulation(c_rows, w_mod, b_mod, w_in):
    n_col = 6
    win_cols = IN_W // n_col
    win_spec = pl.BlockSpec((None, D_MODEL, win_cols), lambda l, j: (l, 0, j))
    return pl.pallas_call(
        _mod_kernel,
        grid=(DEPTH, n_col),
        in_specs=[
            pl.BlockSpec((MOD_ROWS, D_MODEL), lambda l, j: (0, 0)),
            pl.BlockSpec((1, D_MODEL, D_MODEL), lambda l, j: (l, 0, j)),
            pl.BlockSpec((1, 1, D_MODEL), lambda l, j: (l, 0, j)),
            win_spec,
        ],
        out_specs=[pl.BlockSpec((1, MOD_ROWS, D_MODEL), lambda l, j: (l, 0, j)), win_spec],
        out_shape=[jax.ShapeDtypeStruct((DEPTH, MOD_ROWS, 6 * D_MODEL), F32),
                   jax.ShapeDtypeStruct(w_in.shape, BF16)],
        compiler_params=pltpu.CompilerParams(dimension_semantics=("parallel", "parallel")),
        name="modulation",
    )(c_rows, w_mod, b_mod.reshape(DEPTH, 1, 6 * D_MODEL), w_in)


def _store_kv(kn, v, k_ref, vt_ref, rows):
    k_swapped = pltpu.roll(kn, HEAD_DIM, axis=1)
    lo = _half_mask(kn.shape)
    k_ref[0, 0, rows, :] = jnp.where(lo, kn, k_swapped).astype(BF16)
    k_ref[0, 1, rows, :] = jnp.where(lo, k_swapped, kn).astype(BF16)
    vt = v.T
    ones = jnp.ones((HEAD_DIM, vt.shape[1]), F32)
    for hh in range(N_KV_HEADS):
        vt_ref[0, hh, :, rows] = jnp.concatenate(
            [vt[hh * HEAD_DIM:(hh + 1) * HEAD_DIM], ones], axis=0).astype(BF16)


def _kvproj_kernel(x_ref, mod_ref, n1_ref, w_ref, kg_ref, k_ref, vt_ref):
    m = mod_ref[0]
    h = _rms(x_ref[0], n1_ref[...] * (1.0 + m[1:2])) + m[0:1]
    z = jnp.dot(h.astype(BF16), w_ref[...], preferred_element_type=F32)
    kn = _head_rms_slab(z[:, :KV_W], kg_ref[...])
    _store_kv(kn, z[:, KV_W:], k_ref, vt_ref, slice(None))


def _kvproj(layer, x, mod, mod_row_of_batch, norm1, w_in, kg):
    B, N, _ = x.shape
    tm = min(INPROJ_TILE, N)
    assert O_K % (2 * KV_W) == 0 and O_V == O_K + KV_W
    return pl.pallas_call(
        _kvproj_kernel,
        grid=(B, N // tm),
        in_specs=[
            pl.BlockSpec((1, tm, D_MODEL), lambda b, i: (b, i, 0)),
            pl.BlockSpec((None, 1, 6, D_MODEL), lambda b, i: (layer, mod_row_of_batch(b), 0, 0)),
            _layer_spec(layer, 1, D_MODEL),
            pl.BlockSpec((None, D_MODEL, 2 * KV_W), lambda b, i: (layer, 0, O_K // (2 * KV_W))),
            _layer_spec(layer, 1, LANES),
        ],
        out_specs=[
            pl.BlockSpec((1, N_KV_HEADS, tm, KV_W), lambda b, i: (b, 0, i, 0)),
            pl.BlockSpec((1, N_KV_HEADS, 2 * HEAD_DIM, tm), lambda b, i: (b, 0, 0, i)),
        ],
        out_shape=[
            jax.ShapeDtypeStruct((B, N_KV_HEADS, N, KV_W), BF16),
            jax.ShapeDtypeStruct((B, N_KV_HEADS, 2 * HEAD_DIM, N), BF16),
        ],
        compiler_params=pltpu.CompilerParams(
            dimension_semantics=("parallel", "parallel"), vmem_limit_bytes=VMEM_LIMIT),
        name="kvproj",
    )(x, mod, norm1, w_in, kg)

def _inproj_kernel(x_ref, mod_ref, n1_ref, w_ref, qg_ref, kg_ref, sg_ref, cos_ref, sin_ref,
                   q_ref, k_ref, vt_ref, u_ref, vn_ref, p_ref, *, rope):
    m = mod_ref[0]
    tm = x_ref.shape[1]
    sub = min(SUB_TILE, tm)

    gain1 = n1_ref[...] * (1.0 + m[1:2])
    q_gain = qg_ref[...] * (ATTN_SCALE * LOG2E)

    def normalise(r0):
        return (_rms(x_ref[0, r0:r0 + sub, :], gain1) + m[0:1]).astype(BF16)

    def project(h):
        return jnp.dot(h, w_ref[...], preferred_element_type=F32)

    def finish(r0, z):
        rows = slice(r0, r0 + sub)
        if rope:
            cos = cos_ref[rows, :]
            sin = sin_ref[rows, :]

        for j in range(ATTN_W // LANES):
            zq = z[:, O_Q + j * LANES:O_Q + (j + 1) * LANES]
            qn = _head_rms_slab(zq, q_gain)
            if rope:
                qn = _rope_slab(qn, cos, sin)
            q_ref[0, rows, j * LANES:(j + 1) * LANES] = qn.astype(BF16)

        kn = _head_rms_slab(z[:, O_K:O_K + KV_W], kg_ref[...])
        if rope:
            kn = _rope_slab(kn, cos, sin)
        _store_kv(kn, z[:, O_V:O_V + KV_W], k_ref, vt_ref, rows)

        g = _gelu_tanh(z[:, O_G:O_P])
        u_ref[0, rows, :] = g[:, :SGU_W]
        vn_ref[0, rows, :] = _rms(g[:, SGU_W:], sg_ref[...]).astype(BF16)

        p_ref[0, rows, :] = z[:, O_P:]

    starts = list(range(0, tm, sub))
    z = project(normalise(starts[0]))
    for r0, r1 in zip(starts, starts[1:] + [None]):
        z_next = project(normalise(r1)) if r1 is not None else None
        finish(r0, z)
        z = z_next


def _layer_spec(layer, *shape, **kwargs):
    return pl.BlockSpec((None, *shape), lambda *_: (layer,) + (0,) * len(shape), **kwargs)


def _inproj(layer, x, mod, mod_row_of_batch, norm1, w_in, qg, kg, sg, cos, sin, *, rope):
    B, N, _ = x.shape
    tm = min(INPROJ_TILE, N)
    grid = (B, N // tm)
    return pl.pallas_call(
        functools.partial(_inproj_kernel, rope=rope),
        grid=grid,
        in_specs=[
            pl.BlockSpec((1, tm, D_MODEL), lambda b, i: (b, i, 0)),
            pl.BlockSpec((None, 1, 6, D_MODEL), lambda b, i: (layer, mod_row_of_batch(b), 0, 0)),
            _layer_spec(layer, 1, D_MODEL),
            _layer_spec(layer, D_MODEL, IN_W),
            _layer_spec(layer, 1, LANES),
            _layer_spec(layer, 1, LANES),
            _layer_spec(layer, 1, SGU_W),
            pl.BlockSpec((tm, LANES), lambda b, i: (i, 0)),
            pl.BlockSpec((tm, LANES), lambda b, i: (i, 0)),
        ],
        out_specs=[
            pl.BlockSpec((1, tm, ATTN_W), lambda b, i: (b, i, 0)),
            pl.BlockSpec((1, N_KV_HEADS, tm, KV_W), lambda b, i: (b, 0, i, 0)),
            pl.BlockSpec((1, N_KV_HEADS, 2 * HEAD_DIM, tm), lambda b, i: (b, 0, 0, i)),
            pl.BlockSpec((1, tm, SGU_W), lambda b, i: (b, i, 0)),
            pl.BlockSpec((1, tm, SGU_W), lambda b, i: (b, i, 0)),
            pl.BlockSpec((1, tm, POOL_W), lambda b, i: (b, i, 0)),
        ],
        out_shape=[
            jax.ShapeDtypeStruct((B, N, ATTN_W), BF16),
            jax.ShapeDtypeStruct((B, N_KV_HEADS, N, KV_W), BF16),
            jax.ShapeDtypeStruct((B, N_KV_HEADS, 2 * HEAD_DIM, N), BF16),
            jax.ShapeDtypeStruct((B, N, SGU_W), F32),
            jax.ShapeDtypeStruct((B, N, SGU_W), BF16),
            jax.ShapeDtypeStruct((B, N, POOL_W), F32),
        ],
        compiler_params=pltpu.CompilerParams(
            dimension_semantics=("parallel", "parallel"), vmem_limit_bytes=VMEM_LIMIT),
        name="inproj_rope" if rope else "inproj",
    )(x, mod, norm1, w_in, qg, kg, sg, cos, sin)


def _attn_kernel(*refs, layer, n_kv_sets, n_cast):
    bound_ref, q_ref = refs[:2]
    refs = refs[1:]
    k_refs = refs[1:1 + n_kv_sets]
    vt_refs = refs[1 + n_kv_sets:1 + 2 * n_kv_sets]
    n_in = 1 + 2 * n_kv_sets + n_cast
    o_ref = refs[n_in]
    for src, dst in zip(refs[n_in - n_cast:n_in], refs[n_in + 1:]):
        dst[...] = src[...].astype(BF16)
    tq = q_ref.shape[1]
    lo = _half_mask((tq, LANES))
    zero = jnp.zeros((tq, LANES), BF16)
    nt = (((1,), (1,)), ((), ()))

    def attend(bound):
        def scores(idx):
            slab, half, kvh = idx // 2, idx % 2, idx // Q_PER_KV
            qs = q_ref[0, :, slab * LANES:(slab + 1) * LANES]
            qm = jnp.where(lo, qs, zero) if half == 0 else jnp.where(lo, zero, qs)
            st = [lax.dot_general(k[0, kvh], qm, nt, preferred_element_type=F32) for k in k_refs]
            if bound is not None:
                return st, bound
            return st, functools.reduce(
                jnp.maximum, [jnp.max(si, axis=0, keepdims=True) for si in st])

        def weighted_values(idx, st, shift):
            kvh = idx // Q_PER_KV
            acc = functools.reduce(lambda a, b: a + b, [
                jnp.dot(vt[0, kvh], jnp.exp2(si - shift).astype(BF16), preferred_element_type=F32)
                for si, vt in zip(st, vt_refs)])
            return acc[:HEAD_DIM] / acc[HEAD_DIM:HEAD_DIM + 1]

        heads = []
        pending = scores(0)
        for idx in range(N_HEADS):
            nxt = scores(idx + 1) if idx + 1 < N_HEADS else None
            heads.append(weighted_values(idx, *pending))
            pending = nxt
        o_ref[0] = jnp.concatenate(heads, axis=0).T.astype(BF16)

    bound = bound_ref[layer, 0]
    safe = bound <= MAX_SCORE_BOUND

    @pl.when(safe)
    def _():
        attend(bound)

    @pl.when(jnp.logical_not(safe))
    def _():
        attend(None)


def _attention(layer, score_bounds, q, kv_sets, cast_weights=()):
    B, N, _ = q.shape
    tq = min(Q_TILE, N)
    nq = N // tq
    n_steps = B * nq
    in_specs = [pl.BlockSpec(memory_space=pltpu.SMEM),
                pl.BlockSpec((1, tq, ATTN_W), lambda b, i: (b, i, 0))]
    for k, _ in kv_sets:
        in_specs.append(pl.BlockSpec((1, N_KV_HEADS, k.shape[2], KV_W), lambda b, i: (b, 0, 0, 0)))
    for _, vt in kv_sets:
        in_specs.append(pl.BlockSpec((1, N_KV_HEADS, 2 * HEAD_DIM, vt.shape[3]),
                                     lambda b, i: (b, 0, 0, 0)))
    out_specs = [pl.BlockSpec((1, tq, ATTN_W), lambda b, i: (b, i, 0))]
    out_shape = [jax.ShapeDtypeStruct((B, N, ATTN_W), BF16)]
    for w in cast_weights:
        _, rows, cols = w.shape
        per = next(d for d in (1, 2, 4, 8) if rows % (n_steps // d) == 0
                   and (rows // (n_steps // d)) % BF16_ROWS == 0)
        blk = rows // (n_steps // per)
        in_specs.append(pl.BlockSpec((None, blk, cols),
                                     lambda b, i, per=per: (layer, (b * nq + i) // per, 0)))
        out_specs.append(pl.BlockSpec((blk, cols), lambda b, i, per=per: ((b * nq + i) // per, 0)))
        out_shape.append(jax.ShapeDtypeStruct((rows, cols), BF16))
    outs = pl.pallas_call(
        functools.partial(_attn_kernel, layer=layer, n_kv_sets=len(kv_sets),
                          n_cast=len(cast_weights)),
        grid=(B, nq),
        in_specs=in_specs,
        out_specs=out_specs,
        out_shape=out_shape,
        compiler_params=pltpu.CompilerParams(
            dimension_semantics=("arbitrary", "arbitrary"), vmem_limit_bytes=VMEM_LIMIT),
        name=f"attention_{len(kv_sets)}",
    )(score_bounds, q, *[k for k, _ in kv_sets], *[vt for _, vt in kv_sets], *cast_weights)
    return outs if cast_weights else outs[0]


def _mix_stage(i, n_tiles, x_ref, mod_ref, attn_ref, u_ref, vn_ref, p_ref, pprev_ref, pnext_ref,
               ws_ref, bs_ref, pw_ref, ps_ref, wo_ref, n2_ref, seq_len):
    tm = x_ref.shape[1]
    m = mod_ref[0]

    lo = _half_mask((CHUNK, LANES))
    sgu_chunks = []
    for c in range(tm // CHUNK):
        slabs = []
        for j in range(SGU_W // LANES):
            vs = vn_ref[0, c * CHUNK:(c + 1) * CHUNK, j * LANES:(j + 1) * LANES]
            ra = jnp.dot(ws_ref[2 * j], vs, preferred_element_type=F32)
            rb = jnp.dot(ws_ref[2 * j + 1], vs, preferred_element_type=F32)
            slabs.append(jnp.where(lo, ra, rb))
        mixed = jnp.concatenate(slabs, axis=-1) + bs_ref[...]
        sgu_chunks.append((u_ref[0, c * CHUNK:(c + 1) * CHUNK, :] * mixed).astype(BF16))
    sgu = jnp.concatenate(sgu_chunks, axis=0)

    p = p_ref[0]
    prev = jnp.where(i > 0, pprev_ref[0], 0.0)
    nxt = jnp.where(i < n_tiles - 1, pnext_ref[0], 0.0)
    e = jnp.concatenate([prev, p, nxt], axis=0)
    n_ext = tm + 2 * POOL_HALO

    def ahead(a, k):
        return pltpu.roll(a, n_ext - k, axis=0)

    a2 = e + ahead(e, 1)
    a4 = a2 + ahead(a2, 2)
    a8 = a4 + ahead(a4, 4)
    a16 = a8 + ahead(a8, 8)
    s2 = ahead(a2, POOL_HALO - 1)[:tm]
    s4 = ahead(a4, POOL_HALO - 2)[:tm]
    s8 = ahead(a8, POOL_HALO - 4)[:tm]
    s16 = a16[:tm]
    grp = lax.broadcasted_iota(jnp.int32, (tm, POOL_W), 1) >> 6
    win_sum = jnp.where(grp == 0, s2, jnp.where(grp == 1, s4, jnp.where(grp == 2, s8, s16)))
    t = i * tm + lax.broadcasted_iota(jnp.int32, (tm, POOL_W), 0)
    half = jnp.left_shift(1, grp)
    cnt = jnp.minimum(t + half, seq_len) - jnp.maximum(t - half, 0)
    d = win_sum / cnt.astype(F32) - p
    pool = jnp.dot(d.astype(BF16), pw_ref[...], preferred_element_type=F32) * ps_ref[...]


    cat = jnp.concatenate([attn_ref[0], sgu, pool.astype(BF16)], axis=-1)
    mix = jnp.dot(cat, wo_ref[...], preferred_element_type=F32)
    x1 = x_ref[0] + m[2:3] * mix
    h2 = (_rms(x1, n2_ref[...]) * (1.0 + m[4:5]) + m[3:4]).astype(BF16)
    return x1, h2


def _ffn_stage(x1, h2, gate2, wg_ref, wu_ref, wd_ref, fn_ref, final):
    gate = jnp.dot(h2, wg_ref[...], preferred_element_type=F32)
    up = jnp.dot(h2, wu_ref[...], preferred_element_type=F32)
    act = (_silu(gate) * up).astype(BF16)
    x2 = x1 + gate2 * jnp.dot(act, wd_ref[...], preferred_element_type=F32)
    if final:
        x2 = _rms(x2, fn_ref[...])
    return x2


def _mixffn_kernel(x_ref, mod_ref, modp_ref, attn_ref, u_ref, vn_ref, p_ref, pprev_ref, pnext_ref,
                   ws_ref, bs_ref, pw_ref, ps_ref, wo_ref, n2_ref, wg_ref, wu_ref, wd_ref,
                   fn_ref, o_ref, x1_scr, h2_scr, *, seq_len, tiles_per_seq, final):
    s = pl.program_id(0)
    n_steps = pl.num_programs(0)
    i = lax.rem(jnp.minimum(s, n_steps - 2), tiles_per_seq)
    mix_args = (i, tiles_per_seq, x_ref, mod_ref, attn_ref, u_ref, vn_ref, p_ref, pprev_ref,
                pnext_ref, ws_ref, bs_ref, pw_ref, ps_ref, wo_ref, n2_ref, seq_len)
    ffn_args = (wg_ref, wu_ref, wd_ref, fn_ref, final)

    @pl.when(s == 0)
    def _():
        x1, h2 = _mix_stage(*mix_args)
        x1_scr[...] = x1
        h2_scr[...] = h2

    @pl.when(jnp.logical_and(s > 0, s < n_steps - 1))
    def _():
        o_ref[0] = _ffn_stage(x1_scr[...], h2_scr[...], modp_ref[0][5:6], *ffn_args)
        x1, h2 = _mix_stage(*mix_args)
        x1_scr[...] = x1
        h2_scr[...] = h2

    @pl.when(s == n_steps - 1)
    def _():
        o_ref[0] = _ffn_stage(x1_scr[...], h2_scr[...], modp_ref[0][5:6], *ffn_args)


def _mixffn(layer, x, mod, mod_row_of_batch, attn, u, vn, p, ws, bs, pw, ps, wo, n2, wg, wu, wd,
            fn, *, final):
    B, N, _ = x.shape
    tm = min(TOKEN_TILE, N)
    tps = N // tm
    n_tiles = B * tps
    hb = tm // POOL_HALO
    n_hblocks = N // POOL_HALO

    def cur(s):
        t = jnp.minimum(s, n_tiles - 1)
        return t // tps, lax.rem(t, tps)

    def prv(s):
        t = jnp.maximum(s - 1, 0)
        return t // tps, lax.rem(t, tps)

    def tile(w):
        return pl.BlockSpec((1, tm, w), lambda s: (*cur(s), 0))

    def halo_prev(s):
        b, i = cur(s)
        return b, jnp.maximum(i * hb - 1, 0), 0

    def halo_next(s):
        b, i = cur(s)
        return b, jnp.minimum((i + 1) * hb, n_hblocks - 1), 0

    once = dict(pipeline_mode=pl.Buffered(1))
    return pl.pallas_call(
        functools.partial(_mixffn_kernel, seq_len=N, tiles_per_seq=tps, final=final),
        grid=(n_tiles + 1,),
        in_specs=[
            tile(D_MODEL),
            pl.BlockSpec((None, 1, 6, D_MODEL),
                         lambda s: (layer, mod_row_of_batch(cur(s)[0]), 0, 0)),
            pl.BlockSpec((None, 1, 6, D_MODEL),
                         lambda s: (layer, mod_row_of_batch(prv(s)[0]), 0, 0)),
            tile(ATTN_W), tile(SGU_W), tile(SGU_W), tile(POOL_W),
            pl.BlockSpec((1, POOL_HALO, POOL_W), halo_prev),
            pl.BlockSpec((1, POOL_HALO, POOL_W), halo_next),
            _layer_spec(layer, SGU_GROUPS, CHUNK, CHUNK, **once),
            _layer_spec(layer, CHUNK, SGU_W, **once),
            _layer_spec(layer, POOL_W, POOL_W, **once),
            _layer_spec(layer, 1, POOL_W, **once),
            pl.BlockSpec((MIX_W, D_MODEL), lambda s: (0, 0), **once),
            _layer_spec(layer, 1, D_MODEL, **once),
            pl.BlockSpec((D_MODEL, D_FF), lambda s: (0, 0), **once),
            pl.BlockSpec((D_MODEL, D_FF), lambda s: (0, 0), **once),
            pl.BlockSpec((D_FF, D_MODEL), lambda s: (0, 0), **once),
            pl.BlockSpec((1, D_MODEL), lambda s: (0, 0), **once),
        ],
        out_specs=pl.BlockSpec((1, tm, D_MODEL), lambda s: (*prv(s), 0)),
        out_shape=jax.ShapeDtypeStruct((B, N, D_MODEL), F32),
        scratch_shapes=[pltpu.VMEM((tm, D_MODEL), F32), pltpu.VMEM((tm, D_MODEL), BF16)],
        compiler_params=pltpu.CompilerParams(
            dimension_semantics=("arbitrary",), vmem_limit_bytes=VMEM_LIMIT),
        name="mixffn_final" if final else "mixffn",
    )(x, mod, mod, attn, u, vn, p, p, p, ws, bs, pw, ps, wo, n2, wg, wu, wd, fn)


def _rope_tables(n_tokens):
    rows = np.repeat(np.arange(n_tokens // GRID_W, dtype=np.float64), GRID_W)
    cols = np.tile(np.arange(GRID_W, dtype=np.float64), n_tokens // GRID_W)
    inv = ROPE_THETA ** (-np.arange(ROPE_PAIRS_PER_AXIS, dtype=np.float64) / ROPE_PAIRS_PER_AXIS)
    ang = np.concatenate([rows[:, None] * inv, cols[:, None] * inv], axis=-1)
    cos, sin = np.cos(ang), np.sin(ang)
    return (np.tile(cos, (1, 4)).astype(np.float32),
            np.tile(np.concatenate([-sin, sin], axis=-1), (1, 2)).astype(np.float32))


def _block_diag(w):
    d, g, a, b = w.shape
    eye = jnp.eye(g, dtype=w.dtype)
    return (w[:, :, :, None, :] * eye[None, :, None, :, None]).reshape(d, g * a, g * b)


def kernel(x, c, ctx, c_ctx, w_mod, b_mod, norm1, norm2, w_in, q_norm, k_norm, sgu_norm, w_s, b_s,
           pool_w, pool_scale, w_out, w_gate, w_up, w_down, final_norm):
    B, S, _ = x.shape
    L = ctx.shape[1]
    c_rows = jnp.concatenate(
        [c, c_ctx[None, :], jnp.zeros((MOD_ROWS - B - 1, D_MODEL), F32)], axis=0)
    mod, w_in_bf = _modulation(c_rows, w_mod, b_mod, w_in)
    mod = mod.reshape(DEPTH, MOD_ROWS, 6, D_MODEL)
    cos, sin = _rope_tables(S)
    ones = np.ones((L, LANES), np.float32)

    row = lambda a: a[:, None, :]
    proj_w = (
        row(norm1),
        w_in_bf,
        row(jnp.tile(q_norm, (1, 2))),
        row(jnp.tile(k_norm, (1, 2))),
        row(sgu_norm),
    )
    ws_bf = w_s.astype(BF16)
    sgu_bias = jnp.repeat(jnp.swapaxes(b_s, 1, 2), SGU_W // SGU_GROUPS, axis=2)
    pool_bd = _block_diag(pool_w).astype(BF16)
    lat_row = lambda b: b
    ctx_row = lambda b: B
    bounds = (HEAD_DIM * ATTN_SCALE * LOG2E * BOUND_MARGIN
              * jnp.max(jnp.abs(q_norm), axis=1, keepdims=True)
              * jnp.max(jnp.abs(k_norm), axis=1, keepdims=True))

    for l in range(DEPTH):
        last = l == DEPTH - 1
        q, k, vt, u, vn, p = _inproj(l, x, mod, lat_row, *proj_w, cos, sin, rope=True)
        if last:
            kc, vtc = _kvproj(l, ctx, mod, ctx_row, proj_w[0], proj_w[1], proj_w[3])
        else:
            qc, kc, vtc, uc, vnc, pc = _inproj(l, ctx, mod, ctx_row, *proj_w, ones, ones,
                                               rope=False)
        attn, wo_bf, wg_bf, wu_bf, wd_bf = _attention(
            l, bounds, q, [(kc, vtc), (k, vt)], cast_weights=(w_out, w_gate, w_up, w_down))
        mix_w = (ws_bf, sgu_bias, pool_bd, row(pool_scale), wo_bf, row(norm2), wg_bf, wu_bf, wd_bf,
                 final_norm[None, :])
        if not last:
            attn_c = _attention(l, bounds, qc, [(kc, vtc)])
            ctx = _mixffn(l, ctx, mod, ctx_row, attn_c, uc, vnc, pc, *mix_w, final=False)
        x = _mixffn(l, x, mod, lat_row, attn, u, vn, p, *mix_w, final=last)
    return x
```

```python
import functools
import math

import jax
import jax.numpy as jnp
import numpy as np
from jax import lax
from jax.experimental import pallas as pl
from jax.experimental.pallas import tpu as pltpu

F32 = jnp.float32
BF16 = jnp.bfloat16

D_MODEL = 1024
DEPTH = 2
GRID_W = 64
EPS = 1e-6

HEAD_DIM = 64
N_HEADS = 8
N_KV_HEADS = 2
Q_PER_KV = N_HEADS // N_KV_HEADS
ATTN_W = N_HEADS * HEAD_DIM
KV_W = N_KV_HEADS * HEAD_DIM
ROPE_THETA = 10000.0
ROPE_PAIRS_PER_AXIS = HEAD_DIM // 4
ATTN_SCALE = HEAD_DIM ** -0.5
LOG2E = math.log2(math.e)
BOUND_MARGIN = 1.02
MAX_SCORE_BOUND = 50.0

CHUNK = 128
SGU_W = D_MODEL // 4
SGU_GROUPS = 4
POOL_WINDOWS = (2, 4, 8, 16)
POOL_W = D_MODEL // 4
POOL_GROUP_SHIFT = 6
POOL_HALO = 8

MIX_W = ATTN_W + SGU_W + POOL_W
O_Q = 0
O_K = O_Q + ATTN_W
O_V = O_K + KV_W
O_G = O_V + KV_W
O_P = O_G + 2 * SGU_W
IN_W = O_P + POOL_W
D_FF = 2816

LANES = 128
BF16_ROWS = 16
MOD_ROWS = 16
VMEM_LIMIT = 56 * 1024 * 1024

TOKEN_TILE = 512
INPROJ_TILE = 1024
SUB_TILE = 256
Q_TILE = 512


def _half_mask(shape):
    lane = lax.broadcasted_iota(jnp.int32, shape, len(shape) - 1)
    return (lane & HEAD_DIM) == 0


def _rms(x, gain):
    return x * lax.rsqrt(jnp.mean(x * x, axis=-1, keepdims=True) + EPS) * gain


def _head_rms_slab(z, gain):
    lo = _half_mask(z.shape)
    sq = z * z
    s0 = jnp.sum(jnp.where(lo, sq, 0.0), axis=-1, keepdims=True)
    s1 = jnp.sum(jnp.where(lo, 0.0, sq), axis=-1, keepdims=True)
    r0 = lax.rsqrt(s0 * (1.0 / HEAD_DIM) + EPS)
    r1 = lax.rsqrt(s1 * (1.0 / HEAD_DIM) + EPS)
    return z * jnp.where(lo, r0, r1) * gain


def _rope_slab(x, cos, sin_signed):
    lane = lax.broadcasted_iota(jnp.int32, x.shape, 1)
    first = (lane & (HEAD_DIM // 2)) == 0
    up = pltpu.roll(x, LANES - HEAD_DIM // 2, axis=1)
    down = pltpu.roll(x, HEAD_DIM // 2, axis=1)
    partner = jnp.where(first, up, down)
    return x * cos + partner * sin_signed


def _gelu_tanh(x):
    c = math.sqrt(2.0 / math.pi)
    k0 = -2.0 * c * LOG2E
    k1 = k0 * 0.044715
    return x / (1.0 + jnp.exp2(x * (k1 * (x * x) + k0)))


def _silu(x):
    return x / (1.0 + jnp.exp(-x))


def _mod_kernel(c_ref, w_ref, b_ref, win_ref, o_ref, win_bf_ref):
    win_bf_ref[...] = win_ref[...].astype(BF16)

    a = _silu(c_ref[...])
    w = w_ref[0]
    a_hi = a.astype(BF16)
    a_lo = (a - a_hi.astype(F32)).astype(BF16)
    w_hi = w.astype(BF16)
    w_lo = (w - w_hi.astype(F32)).astype(BF16)
    dot = functools.partial(jnp.dot, preferred_element_type=F32)
    o_ref[0] = dot(a_hi, w_hi) + (dot(a_lo, w_hi) + dot(a_hi, w_lo)) + b_ref[0]


def _modulation(c_rows, w_mod, b_mod, w_in):
    n_col = 6
    win_cols = IN_W // n_col
    win_spec = pl.BlockSpec((None, D_MODEL, win_cols), lambda l, j: (l, 0, j))
    return pl.pallas_call(
        _mod_kernel,
        grid=(DEPTH, n_col),
        in_specs=[
            pl.BlockSpec((MOD_ROWS, D_MODEL), lambda l, j: (0, 0)),
            pl.BlockSpec((1, D_MODEL, D_MODEL), lambda l, j: (l, 0, j)),
            pl.BlockSpec((1, 1, D_MODEL), lambda l, j: (l, 0, j)),
            win_spec,
        ],
        out_specs=[pl.BlockSpec((1, MOD_ROWS, D_MODEL), lambda l, j: (l, 0, j)), win_spec],
        out_shape=[jax.ShapeDtypeStruct((DEPTH, MOD_ROWS, 6 * D_MODEL), F32),
                   jax.ShapeDtypeStruct(w_in.shape, BF16)],
        compiler_params=pltpu.CompilerParams(dimension_semantics=("parallel", "parallel")),
        name="modulation",
    )(c_rows, w_mod, b_mod.reshape(DEPTH, 1, 6 * D_MODEL), w_in)


def _store_kv(kn, v, k_ref, vt_ref, rows):
    k_swapped = pltpu.roll(kn, HEAD_DIM, axis=1)
    lo = _half_mask(kn.shape)
    k_ref[0, 0, rows, :] = jnp.where(lo, kn, k_swapped).astype(BF16)
    k_ref[0, 1, rows, :] = jnp.where(lo, k_swapped, kn).astype(BF16)
    vt = v.T
    ones = jnp.ones((HEAD_DIM, vt.shape[1]), F32)
    for hh in range(N_KV_HEADS):
        vt_ref[0, hh, :, rows] = jnp.concatenate(
            [vt[hh * HEAD_DIM:(hh + 1) * HEAD_DIM], ones], axis=0).astype(BF16)


def _kvproj_kernel(x_ref, mod_ref, n1_ref, w_ref, kg_ref, k_ref, vt_ref):
    m = mod_ref[0]
    h = _rms(x_ref[0], n1_ref[...] * (1.0 + m[1:2])) + m[0:1]
    z = jnp.dot(h.astype(BF16), w_ref[...], preferred_element_type=F32)
    kn = _head_rms_slab(z[:, :KV_W], kg_ref[...])
    _store_kv(kn, z[:, KV_W:], k_ref, vt_ref, slice(None))


def _kvproj(layer, x, mod, mod_row_of_batch, norm1, w_in, kg):
    B, N, _ = x.shape
    tm = min(INPROJ_TILE, N)
    assert O_K % (2 * KV_W) == 0 and O_V == O_K + KV_W
    return pl.pallas_call(
        _kvproj_kernel,
        grid=(B, N // tm),
        in_specs=[
            pl.BlockSpec((1, tm, D_MODEL), lambda b, i: (b, i, 0)),
            pl.BlockSpec((None, 1, 6, D_MODEL), lambda b, i: (layer, mod_row_of_batch(b), 0, 0)),
            _layer_spec(layer, 1, D_MODEL),
            pl.BlockSpec((None, D_MODEL, 2 * KV_W), lambda b, i: (layer, 0, O_K // (2 * KV_W))),
            _layer_spec(layer, 1, LANES),
        ],
        out_specs=[
            pl.BlockSpec((1, N_KV_HEADS, tm, KV_W), lambda b, i: (b, 0, i, 0)),
            pl.BlockSpec((1, N_KV_HEADS, 2 * HEAD_DIM, tm), lambda b, i: (b, 0, 0, i)),
        ],
        out_shape=[
            jax.ShapeDtypeStruct((B, N_KV_HEADS, N, KV_W), BF16),
            jax.ShapeDtypeStruct((B, N_KV_HEADS, 2 * HEAD_DIM, N), BF16),
        ],
        compiler_params=pltpu.CompilerParams(
            dimension_semantics=("parallel", "parallel"), vmem_limit_bytes=VMEM_LIMIT),
        name="kvproj",
    )(x, mod, norm1, w_in, kg)


def _inproj_kernel(x_ref, mod_ref, n1_ref, w_ref, qg_ref, kg_ref, sg_ref, cos_ref, sin_ref,
                   q_ref, k_ref, vt_ref, u_ref, vn_ref, p_ref, *, rope):
    m = mod_ref[0]
    tm = x_ref.shape[1]
    sub = min(SUB_TILE, tm)

    gain1 = n1_ref[...] * (1.0 + m[1:2])
    q_gain = qg_ref[...] * (ATTN_SCALE * LOG2E)

    def normalise(r0):
        return (_rms(x_ref[0, r0:r0 + sub, :], gain1) + m[0:1]).astype(BF16)

    def project(h):
        return jnp.dot(h, w_ref[...], preferred_element_type=F32)

    def finish(r0, z):
        rows = slice(r0, r0 + sub)
        if rope:
            cos = cos_ref[rows, :]
            sin = sin_ref[rows, :]

        for j in range(ATTN_W // LANES):
            zq = z[:, O_Q + j * LANES:O_Q + (j + 1) * LANES]
            qn = _head_rms_slab(zq, q_gain)
            if rope:
                qn = _rope_slab(qn, cos, sin)
            q_ref[0, rows, j * LANES:(j + 1) * LANES] = qn.astype(BF16)

        kn = _head_rms_slab(z[:, O_K:O_K + KV_W], kg_ref[...])
        if rope:
            kn = _rope_slab(kn, cos, sin)
        _store_kv(kn, z[:, O_V:O_V + KV_W], k_ref, vt_ref, rows)

        g = _gelu_tanh(z[:, O_G:O_P])
        u_ref[0, rows, :] = g[:, :SGU_W]
        vn_ref[0, rows, :] = _rms(g[:, SGU_W:], sg_ref[...]).astype(BF16)

        p_ref[0, rows, :] = z[:, O_P:]

    starts = list(range(0, tm, sub))
    z = project(normalise(starts[0]))
    for r0, r1 in zip(starts, starts[1:] + [None]):
        z_next = project(normalise(r1)) if r1 is not None else None
        finish(r0, z)
        z = z_next


def _layer_spec(layer, *shape, **kwargs):
    return pl.BlockSpec((None, *shape), lambda *_: (layer,) + (0,) * len(shape), **kwargs)


def _inproj(layer, x, mod, mod_row_of_batch, norm1, w_in, qg, kg, sg, cos, sin, *, rope):
    B, N, _ = x.shape
    tm = min(INPROJ_TILE, N)
    grid = (B, N // tm)
    return pl.pallas_call(
        functools.partial(_inproj_kernel, rope=rope),
        grid=grid,
        in_specs=[
            pl.BlockSpec((1, tm, D_MODEL), lambda b, i: (b, i, 0)),
            pl.BlockSpec((None, 1, 6, D_MODEL), lambda b, i: (layer, mod_row_of_batch(b), 0, 0)),
            _layer_spec(layer, 1, D_MODEL),
            _layer_spec(layer, D_MODEL, IN_W),
            _layer_spec(layer, 1, LANES),
            _layer_spec(layer, 1, LANES),
            _layer_spec(layer, 1, SGU_W),
            pl.BlockSpec((tm, LANES), lambda b, i: (i, 0)),
            pl.BlockSpec((tm, LANES), lambda b, i: (i, 0)),
        ],
        out_specs=[
            pl.BlockSpec((1, tm, ATTN_W), lambda b, i: (b, i, 0)),
            pl.BlockSpec((1, N_KV_HEADS, tm, KV_W), lambda b, i: (b, 0, i, 0)),
            pl.BlockSpec((1, N_KV_HEADS, 2 * HEAD_DIM, tm), lambda b, i: (b, 0, 0, i)),
            pl.BlockSpec((1, tm, SGU_W), lambda b, i: (b, i, 0)),
            pl.BlockSpec((1, tm, SGU_W), lambda b, i: (b, i, 0)),
            pl.BlockSpec((1, tm, POOL_W), lambda b, i: (b, i, 0)),
        ],
        out_shape=[
            jax.ShapeDtypeStruct((B, N, ATTN_W), BF16),
            jax.ShapeDtypeStruct((B, N_KV_HEADS, N, KV_W), BF16),
            jax.ShapeDtypeStruct((B, N_KV_HEADS, 2 * HEAD_DIM, N), BF16),
            jax.ShapeDtypeStruct((B, N, SGU_W), F32),
            jax.ShapeDtypeStruct((B, N, SGU_W), BF16),
            jax.ShapeDtypeStruct((B, N, POOL_W), F32),
        ],
        compiler_params=pltpu.CompilerParams(
            dimension_semantics=("parallel", "parallel"), vmem_limit_bytes=VMEM_LIMIT),
        name="inproj_rope" if rope else "inproj",
    )(x, mod, norm1, w_in, qg, kg, sg, cos, sin)


def _attn_kernel(*refs, layer, n_kv_sets, n_cast):
    bound_ref, q_ref = refs[:2]
    refs = refs[1:]
    k_refs = refs[1:1 + n_kv_sets]
    vt_refs = refs[1 + n_kv_sets:1 + 2 * n_kv_sets]
    n_in = 1 + 2 * n_kv_sets + n_cast
    o_ref = refs[n_in]
    for src, dst in zip(refs[n_in - n_cast:n_in], refs[n_in + 1:]):
        dst[...] = src[...].astype(BF16)
    tq = q_ref.shape[1]
    lo = _half_mask((tq, LANES))
    zero = jnp.zeros((tq, LANES), BF16)
    nt = (((1,), (1,)), ((), ()))

    def attend(bound):
        def scores(idx):
            slab, half, kvh = idx // 2, idx % 2, idx // Q_PER_KV
            qs = q_ref[0, :, slab * LANES:(slab + 1) * LANES]
            qm = jnp.where(lo, qs, zero) if half == 0 else jnp.where(lo, zero, qs)
            st = [lax.dot_general(k[0, kvh], qm, nt, preferred_element_type=F32) for k in k_refs]
            if bound is not None:
                return st, bound
            return st, functools.reduce(
                jnp.maximum, [jnp.max(si, axis=0, keepdims=True) for si in st])

        def weighted_values(idx, st, shift):
            kvh = idx // Q_PER_KV
            acc = functools.reduce(lambda a, b: a + b, [
                jnp.dot(vt[0, kvh], jnp.exp2(si - shift).astype(BF16), preferred_element_type=F32)
                for si, vt in zip(st, vt_refs)])
            return acc[:HEAD_DIM] / acc[HEAD_DIM:HEAD_DIM + 1]

        heads = []
        pending = scores(0)
        for idx in range(N_HEADS):
            nxt = scores(idx + 1) if idx + 1 < N_HEADS else None
            heads.append(weighted_values(idx, *pending))
            pending = nxt
        o_ref[0] = jnp.concatenate(heads, axis=0).T.astype(BF16)

    bound = bound_ref[layer, 0]
    safe = bound <= MAX_SCORE_BOUND

    @pl.when(safe)
    def _():
        attend(bound)

    @pl.when(jnp.logical_not(safe))
    def _():
        attend(None)


def _attention(layer, score_bounds, q, kv_sets, cast_weights=()):
    B, N, _ = q.shape
    tq = min(Q_TILE, N)
    nq = N // tq
    n_steps = B * nq
    in_specs = [pl.BlockSpec(memory_space=pltpu.SMEM),
                pl.BlockSpec((1, tq, ATTN_W), lambda b, i: (b, i, 0))]
    for k, _ in kv_sets:
        in_specs.append(pl.BlockSpec((1, N_KV_HEADS, k.shape[2], KV_W), lambda b, i: (b, 0, 0, 0)))
    for _, vt in kv_sets:
        in_specs.append(pl.BlockSpec((1, N_KV_HEADS, 2 * HEAD_DIM, vt.shape[3]),
                                     lambda b, i: (b, 0, 0, 0)))
    out_specs = [pl.BlockSpec((1, tq, ATTN_W), lambda b, i: (b, i, 0))]
    out_shape = [jax.ShapeDtypeStruct((B, N, ATTN_W), BF16)]
    for w in cast_weights:
        _, rows, cols = w.shape
        per = next(d for d in (1, 2, 4, 8) if rows % (n_steps // d) == 0
                   and (rows // (n_steps // d)) % BF16_ROWS == 0)
        blk = rows // (n_steps // per)
        in_specs.append(pl.BlockSpec((None, blk, cols),
                                     lambda b, i, per=per: (layer, (b * nq + i) // per, 0)))
        out_specs.append(pl.BlockSpec((blk, cols), lambda b, i, per=per: ((b * nq + i) // per, 0)))
        out_shape.append(jax.ShapeDtypeStruct((rows, cols), BF16))
    outs = pl.pallas_call(
        functools.partial(_attn_kernel, layer=layer, n_kv_sets=len(kv_sets),
                          n_cast=len(cast_weights)),
        grid=(B, nq),
        in_specs=in_specs,
        out_specs=out_specs,
        out_shape=out_shape,
        compiler_params=pltpu.CompilerParams(
            dimension_semantics=("arbitrary", "arbitrary"), vmem_limit_bytes=VMEM_LIMIT),
        name=f"attention_{len(kv_sets)}",
    )(score_bounds, q, *[k for k, _ in kv_sets], *[vt for _, vt in kv_sets], *cast_weights)
    return outs if cast_weights else outs[0]


def _mix_stage(i, n_tiles, x_ref, mod_ref, attn_ref, u_ref, vn_ref, p_ref, pprev_ref, pnext_ref,
               ws_ref, bs_ref, pw_ref, ps_ref, wo_ref, n2_ref, seq_len):
    tm = x_ref.shape[1]
    m = mod_ref[0]

    lo = _half_mask((CHUNK, LANES))
    sgu_chunks = []
    for c in range(tm // CHUNK):
        slabs = []
        for j in range(SGU_W // LANES):
            vs = vn_ref[0, c * CHUNK:(c + 1) * CHUNK, j * LANES:(j + 1) * LANES]
            ra = jnp.dot(ws_ref[2 * j], vs, preferred_element_type=F32)
            rb = jnp.dot(ws_ref[2 * j + 1], vs, preferred_element_type=F32)
            slabs.append(jnp.where(lo, ra, rb))
        mixed = jnp.concatenate(slabs, axis=-1) + bs_ref[...]
        sgu_chunks.append((u_ref[0, c * CHUNK:(c + 1) * CHUNK, :] * mixed).astype(BF16))
    sgu = jnp.concatenate(sgu_chunks, axis=0)

    p = p_ref[0]
    prev = jnp.where(i > 0, pprev_ref[0], 0.0)
    nxt = jnp.where(i < n_tiles - 1, pnext_ref[0], 0.0)
    e = jnp.concatenate([prev, p, nxt], axis=0)
    n_ext = tm + 2 * POOL_HALO

    def ahead(a, k):
        return pltpu.roll(a, n_ext - k, axis=0)

    a2 = e + ahead(e, 1)
    a4 = a2 + ahead(a2, 2)
    a8 = a4 + ahead(a4, 4)
    a16 = a8 + ahead(a8, 8)
    s2 = ahead(a2, POOL_HALO - 1)[:tm]
    s4 = ahead(a4, POOL_HALO - 2)[:tm]
    s8 = ahead(a8, POOL_HALO - 4)[:tm]
    s16 = a16[:tm]
    grp = lax.broadcasted_iota(jnp.int32, (tm, POOL_W), 1) >> POOL_GROUP_SHIFT
    win_sum = jnp.where(grp == 0, s2, jnp.where(grp == 1, s4, jnp.where(grp == 2, s8, s16)))
    t = i * tm + lax.broadcasted_iota(jnp.int32, (tm, POOL_W), 0)
    half = jnp.left_shift(1, grp)
    cnt = jnp.minimum(t + half, seq_len) - jnp.maximum(t - half, 0)
    d = win_sum / cnt.astype(F32) - p
    pool = jnp.dot(d.astype(BF16), pw_ref[...], preferred_element_type=F32) * ps_ref[...]

    cat = jnp.concatenate([attn_ref[0], sgu, pool.astype(BF16)], axis=-1)
    mix = jnp.dot(cat, wo_ref[...], preferred_element_type=F32)
    x1 = x_ref[0] + m[2:3] * mix
    h2 = (_rms(x1, n2_ref[...]) * (1.0 + m[4:5]) + m[3:4]).astype(BF16)
    return x1, h2


def _ffn_stage(x1, h2, gate2, wg_ref, wu_ref, wd_ref, fn_ref, final):
    gate = jnp.dot(h2, wg_ref[...], preferred_element_type=F32)
    up = jnp.dot(h2, wu_ref[...], preferred_element_type=F32)
    act = (_silu(gate) * up).astype(BF16)
    x2 = x1 + gate2 * jnp.dot(act, wd_ref[...], preferred_element_type=F32)
    if final:
        x2 = _rms(x2, fn_ref[...])
    return x2


def _mixffn_kernel(x_ref, mod_ref, modp_ref, attn_ref, u_ref, vn_ref, p_ref, pprev_ref, pnext_ref,
                   ws_ref, bs_ref, pw_ref, ps_ref, wo_ref, n2_ref, wg_ref, wu_ref, wd_ref,
                   fn_ref, o_ref, x1_scr, h2_scr, *, seq_len, tiles_per_seq, final):
    s = pl.program_id(0)
    n_steps = pl.num_programs(0)
    i = lax.rem(jnp.minimum(s, n_steps - 2), tiles_per_seq)
    mix_args = (i, tiles_per_seq, x_ref, mod_ref, attn_ref, u_ref, vn_ref, p_ref, pprev_ref,
                pnext_ref, ws_ref, bs_ref, pw_ref, ps_ref, wo_ref, n2_ref, seq_len)
    ffn_args = (wg_ref, wu_ref, wd_ref, fn_ref, final)

    @pl.when(s == 0)
    def _():
        x1, h2 = _mix_stage(*mix_args)
        x1_scr[...] = x1
        h2_scr[...] = h2

    @pl.when(jnp.logical_and(s > 0, s < n_steps - 1))
    def _():
        o_ref[0] = _ffn_stage(x1_scr[...], h2_scr[...], modp_ref[0][5:6], *ffn_args)
        x1, h2 = _mix_stage(*mix_args)
        x1_scr[...] = x1
        h2_scr[...] = h2

    @pl.when(s == n_steps - 1)
    def _():
        o_ref[0] = _ffn_stage(x1_scr[...], h2_scr[...], modp_ref[0][5:6], *ffn_args)


def _mixffn(layer, x, mod, mod_row_of_batch, attn, u, vn, p, ws, bs, pw, ps, wo, n2, wg, wu, wd,
            fn, *, final):
    B, N, _ = x.shape
    tm = min(TOKEN_TILE, N)
    tps = N // tm
    n_tiles = B * tps
    hb = tm // POOL_HALO
    n_hblocks = N // POOL_HALO

    def cur(s):
        t = jnp.minimum(s, n_tiles - 1)
        return t // tps, lax.rem(t, tps)

    def prv(s):
        t = jnp.maximum(s - 1, 0)
        return t // tps, lax.rem(t, tps)

    def tile(w):
        return pl.BlockSpec((1, tm, w), lambda s: (*cur(s), 0))

    def halo_prev(s):
        b, i = cur(s)
        return b, jnp.maximum(i * hb - 1, 0), 0

    def halo_next(s):
        b, i = cur(s)
        return b, jnp.minimum((i + 1) * hb, n_hblocks - 1), 0

    once = dict(pipeline_mode=pl.Buffered(1))
    return pl.pallas_call(
        functools.partial(_mixffn_kernel, seq_len=N, tiles_per_seq=tps, final=final),
        grid=(n_tiles + 1,),
        in_specs=[
            tile(D_MODEL),
            pl.BlockSpec((None, 1, 6, D_MODEL),
                         lambda s: (layer, mod_row_of_batch(cur(s)[0]), 0, 0)),
            pl.BlockSpec((None, 1, 6, D_MODEL),
                         lambda s: (layer, mod_row_of_batch(prv(s)[0]), 0, 0)),
            tile(ATTN_W), tile(SGU_W), tile(SGU_W), tile(POOL_W),
            pl.BlockSpec((1, POOL_HALO, POOL_W), halo_prev),
            pl.BlockSpec((1, POOL_HALO, POOL_W), halo_next),
            _layer_spec(layer, SGU_GROUPS, CHUNK, CHUNK, **once),
            _layer_spec(layer, CHUNK, SGU_W, **once),
            _layer_spec(layer, POOL_W, POOL_W, **once),
            _layer_spec(layer, 1, POOL_W, **once),
            pl.BlockSpec((MIX_W, D_MODEL), lambda s: (0, 0), **once),
            _layer_spec(layer, 1, D_MODEL, **once),
            pl.BlockSpec((D_MODEL, D_FF), lambda s: (0, 0), **once),
            pl.BlockSpec((D_MODEL, D_FF), lambda s: (0, 0), **once),
            pl.BlockSpec((D_FF, D_MODEL), lambda s: (0, 0), **once),
            pl.BlockSpec((1, D_MODEL), lambda s: (0, 0), **once),
        ],
        out_specs=pl.BlockSpec((1, tm, D_MODEL), lambda s: (*prv(s), 0)),
        out_shape=jax.ShapeDtypeStruct((B, N, D_MODEL), F32),
        scratch_shapes=[pltpu.VMEM((tm, D_MODEL), F32), pltpu.VMEM((tm, D_MODEL), BF16)],
        compiler_params=pltpu.CompilerParams(
            dimension_semantics=("arbitrary",), vmem_limit_bytes=VMEM_LIMIT),
        name="mixffn_final" if final else "mixffn",
    )(x, mod, mod, attn, u, vn, p, p, p, ws, bs, pw, ps, wo, n2, wg, wu, wd, fn)


def _rope_tables(n_tokens):
    rows = np.repeat(np.arange(n_tokens // GRID_W, dtype=np.float64), GRID_W)
    cols = np.tile(np.arange(GRID_W, dtype=np.float64), n_tokens // GRID_W)
    inv = ROPE_THETA ** (-np.arange(ROPE_PAIRS_PER_AXIS, dtype=np.float64) / ROPE_PAIRS_PER_AXIS)
    ang = np.concatenate([rows[:, None] * inv, cols[:, None] * inv], axis=-1)
    cos, sin = np.cos(ang), np.sin(ang)
    return (np.tile(cos, (1, 4)).astype(np.float32),
            np.tile(np.concatenate([-sin, sin], axis=-1), (1, 2)).astype(np.float32))


def _block_diag(w):
    d, g, a, b = w.shape
    eye = jnp.eye(g, dtype=w.dtype)
    return (w[:, :, :, None, :] * eye[None, :, None, :, None]).reshape(d, g * a, g * b)


def kernel(x, c, ctx, c_ctx, w_mod, b_mod, norm1, norm2, w_in, q_norm, k_norm, sgu_norm, w_s, b_s,
           pool_w, pool_scale, w_out, w_gate, w_up, w_down, final_norm):
    B, S, _ = x.shape
    L = ctx.shape[1]
    c_rows = jnp.concatenate(
        [c, c_ctx[None, :], jnp.zeros((MOD_ROWS - B - 1, D_MODEL), F32)], axis=0)
    mod, w_in_bf = _modulation(c_rows, w_mod, b_mod, w_in)
    mod = mod.reshape(DEPTH, MOD_ROWS, 6, D_MODEL)
    cos, sin = _rope_tables(S)
    ones = np.ones((L, LANES), np.float32)

    row = lambda a: a[:, None, :]
    proj_w = (
        row(norm1),
        w_in_bf,
        row(jnp.tile(q_norm, (1, 2))),
        row(jnp.tile(k_norm, (1, 2))),
        row(sgu_norm),
    )
    ws_bf = w_s.astype(BF16)
    sgu_bias = jnp.repeat(jnp.swapaxes(b_s, 1, 2), SGU_W // SGU_GROUPS, axis=2)
    pool_bd = _block_diag(pool_w).astype(BF16)
    lat_row = lambda b: b
    ctx_row = lambda b: B
    bounds = (HEAD_DIM * ATTN_SCALE * LOG2E * BOUND_MARGIN
              * jnp.max(jnp.abs(q_norm), axis=1, keepdims=True)
              * jnp.max(jnp.abs(k_norm), axis=1, keepdims=True))

    for l in range(DEPTH):
        last = l == DEPTH - 1
        q, k, vt, u, vn, p = _inproj(l, x, mod, lat_row, *proj_w, cos, sin, rope=True)
        if last:
            kc, vtc = _kvproj(l, ctx, mod, ctx_row, proj_w[0], proj_w[1], proj_w[3])
        else:
            qc, kc, vtc, uc, vnc, pc = _inproj(l, ctx, mod, ctx_row, *proj_w, ones, ones,
                                               rope=False)
        attn, wo_bf, wg_bf, wu_bf, wd_bf = _attention(
            l, bounds, q, [(kc, vtc), (k, vt)], cast_weights=(w_out, w_gate, w_up, w_down))
        mix_w = (ws_bf, sgu_bias, pool_bd, row(pool_scale), wo_bf, row(norm2), wg_bf, wu_bf, wd_bf,
                 final_norm[None, :])
        if not last:
            attn_c = _attention(l, bounds, qc, [(kc, vtc)])
            ctx = _mixffn(l, ctx, mod, ctx_row, attn_c, uc, vnc, pc, *mix_w, final=False)
        x = _mixffn(l, x, mod, lat_row, attn, u, vn, p, *mix_w, final=last)
    return x
```

```python
import functools
import math

import jax
import jax.numpy as jnp
import numpy as np
from jax import lax
from jax.experimental import pallas as pl
from jax.experimental.pallas import tpu as pltpu

F32 = jnp.float32
BF16 = jnp.bfloat16

D_MODEL = 1024
DEPTH = 2
GRID_W = 64
EPS = 1e-6

HEAD_DIM = 64
N_HEADS = 8
N_KV_HEADS = 2
Q_PER_KV = N_HEADS // N_KV_HEADS
ATTN_W = N_HEADS * HEAD_DIM
KV_W = N_KV_HEADS * HEAD_DIM
ROPE_THETA = 10000.0
ROPE_PAIRS_PER_AXIS = HEAD_DIM // 4
ATTN_SCALE = HEAD_DIM ** -0.5
LOG2E = math.log2(math.e)
BOUND_MARGIN = 1.02
MAX_SCORE_BOUND = 50.0

CHUNK = 128
SGU_W = D_MODEL // 4
SGU_GROUPS = 4
POOL_WINDOWS = (2, 4, 8, 16)
POOL_W = D_MODEL // 4
POOL_HALO = 8

MIX_W = ATTN_W + SGU_W + POOL_W
O_Q = 0
O_K = O_Q + ATTN_W
O_V = O_K + KV_W
O_G = O_V + KV_W
O_P = O_G + 2 * SGU_W
IN_W = O_P + POOL_W
D_FF = 2816

LANES = 128
BF16_ROWS = 16
MOD_ROWS = 16
VMEM_LIMIT = 56 * 1024 * 1024

TOKEN_TILE = 512
INPROJ_TILE = 1024
SUB_TILE = 256
Q_TILE = 512


def _half_mask(shape):
    lane = lax.broadcasted_iota(jnp.int32, shape, len(shape) - 1)
    return (lane & HEAD_DIM) == 0


def _rms(x, gain):
    return x * lax.rsqrt(jnp.mean(x * x, axis=-1, keepdims=True) + EPS) * gain


def _head_rms_slab(z, gain):
    lo = _half_mask(z.shape)
    sq = z * z
    s0 = jnp.sum(jnp.where(lo, sq, 0.0), axis=-1, keepdims=True)
    s1 = jnp.sum(jnp.where(lo, 0.0, sq), axis=-1, keepdims=True)
    r0 = lax.rsqrt(s0 * (1.0 / HEAD_DIM) + EPS)
    r1 = lax.rsqrt(s1 * (1.0 / HEAD_DIM) + EPS)
    return z * jnp.where(lo, r0, r1) * gain


def _rope_slab(x, cos, sin_signed):
    lane = lax.broadcasted_iota(jnp.int32, x.shape, 1)
    first = (lane & (HEAD_DIM // 2)) == 0
    up = pltpu.roll(x, LANES - HEAD_DIM // 2, axis=1)
    down = pltpu.roll(x, HEAD_DIM // 2, axis=1)
    partner = jnp.where(first, up, down)
    return x * cos + partner * sin_signed


def _gelu_tanh(x):
    c = math.sqrt(2.0 / math.pi)
    k0 = -2.0 * c * LOG2E
    k1 = k0 * 0.044715
    return x / (1.0 + jnp.exp2(x * (k1 * (x * x) + k0)))


def _silu(x):
    return x / (1.0 + jnp.exp(-x))


def _mod_kernel(c_ref, w_ref, b_ref, win_ref, o_ref, win_bf_ref):
    win_bf_ref[...] = win_ref[...].astype(BF16)

    a = _silu(c_ref[...])
    w = w_ref[0]
    a_hi = a.astype(BF16)
    a_lo = (a - a_hi.astype(F32)).astype(BF16)
    w_hi = w.astype(BF16)
    w_lo = (w - w_hi.astype(F32)).astype(BF16)
    dot = functools.partial(jnp.dot, preferred_element_type=F32)
    o_ref[0] = dot(a_hi, w_hi) + (dot(a_lo, w_hi) + dot(a_hi, w_lo)) + b_ref[0]


def _modulation(c_rows, w_mod, b_mod, w_in):
    n_col = 6
    win_cols = IN_W // n_col
    win_spec = pl.BlockSpec((None, D_MODEL, win_cols), lambda l, j: (l, 0, j))
    return pl.pallas_call(
        _mod_kernel,
        grid=(DEPTH, n_col),
        in_specs=[
            pl.BlockSpec((MOD_ROWS, D_MODEL), lambda l, j: (0, 0)),
            pl.BlockSpec((1, D_MODEL, D_MODEL), lambda l, j: (l, 0, j)),
            pl.BlockSpec((1, 1, D_MODEL), lambda l, j: (l, 0, j)),
            win_spec,
        ],
        out_specs=[pl.BlockSpec((1, MOD_ROWS, D_MODEL), lambda l, j: (l, 0, j)), win_spec],
        out_shape=[jax.ShapeDtypeStruct((DEPTH, MOD_ROWS, 6 * D_MODEL), F32),
                   jax.ShapeDtypeStruct(w_in.shape, BF16)],
        compiler_params=pltpu.CompilerParams(dimension_semantics=("parallel", "parallel")),
        name="modulation",
    )(c_rows, w_mod, b_mod.reshape(DEPTH, 1, 6 * D_MODEL), w_in)


def _store_kv(kn, v, k_ref, vt_ref, rows, bi=0):
    k_swapped = pltpu.roll(kn, HEAD_DIM, axis=1)
    lo = _half_mask(kn.shape)
    k_ref[bi, 0, rows, :] = jnp.where(lo, kn, k_swapped).astype(BF16)
    k_ref[bi, 1, rows, :] = jnp.where(lo, k_swapped, kn).astype(BF16)
    vt = v.T
    ones = jnp.ones((HEAD_DIM, vt.shape[1]), F32)
    for hh in range(N_KV_HEADS):
        vt_ref[bi, hh, :, rows] = jnp.concatenate(
            [vt[hh * HEAD_DIM:(hh + 1) * HEAD_DIM], ones], axis=0).astype(BF16)


def _kvproj_kernel(x_ref, mod_ref, n1_ref, w_ref, kg_ref, k_ref, vt_ref):
    m = mod_ref[0]
    h = _rms(x_ref[0], n1_ref[...] * (1.0 + m[1:2])) + m[0:1]
    z = jnp.dot(h.astype(BF16), w_ref[...], preferred_element_type=F32)
    kn = _head_rms_slab(z[:, :KV_W], kg_ref[...])
    _store_kv(kn, z[:, KV_W:], k_ref, vt_ref, slice(None))


def _kvproj(layer, x, mod, mod_row_of_batch, norm1, w_in, kg):
    B, N, _ = x.shape
    tm = min(INPROJ_TILE, N)
    assert O_K % (2 * KV_W) == 0 and O_V == O_K + KV_W
    return pl.pallas_call(
        _kvproj_kernel,
        grid=(B, N // tm),
        in_specs=[
            pl.BlockSpec((1, tm, D_MODEL), lambda b, i: (b, i, 0)),
            pl.BlockSpec((None, 1, 6, D_MODEL), lambda b, i: (layer, mod_row_of_batch(b), 0, 0)),
            _layer_spec(layer, 1, D_MODEL),
            pl.BlockSpec((None, D_MODEL, 2 * KV_W), lambda b, i: (layer, 0, O_K // (2 * KV_W))),
            _layer_spec(layer, 1, LANES),
        ],
        out_specs=[
            pl.BlockSpec((1, N_KV_HEADS, tm, KV_W), lambda b, i: (b, 0, i, 0)),
            pl.BlockSpec((1, N_KV_HEADS, 2 * HEAD_DIM, tm), lambda b, i: (b, 0, 0, i)),
        ],
        out_shape=[
            jax.ShapeDtypeStruct((B, N_KV_HEADS, N, KV_W), BF16),
            jax.ShapeDtypeStruct((B, N_KV_HEADS, 2 * HEAD_DIM, N), BF16),
        ],
        compiler_params=pltpu.CompilerParams(
            dimension_semantics=("parallel", "parallel"), vmem_limit_bytes=VMEM_LIMIT),
        name="kvproj",
    )(x, mod, norm1, w_in, kg)


def _inproj_kernel(x_ref, mod_ref, n1_ref, w_ref, qg_ref, kg_ref, sg_ref, cos_ref, sin_ref,
                   q_ref, k_ref, vt_ref, u_ref, vn_ref, p_ref, *, rope):
    m = mod_ref[0]
    tm = x_ref.shape[1]
    sub = min(SUB_TILE, tm)

    gain1 = n1_ref[...] * (1.0 + m[1:2])
    q_gain = qg_ref[...] * (ATTN_SCALE * LOG2E)

    def normalise(at):
        bi, r0 = at
        return (_rms(x_ref[bi, r0:r0 + sub, :], gain1) + m[0:1]).astype(BF16)

    def project(h):
        return jnp.dot(h, w_ref[...], preferred_element_type=F32)

    def finish(at, z):
        bi, r0 = at
        rows = slice(r0, r0 + sub)
        if rope:
            cos = cos_ref[rows, :]
            sin = sin_ref[rows, :]

        for j in range(ATTN_W // LANES):
            zq = z[:, O_Q + j * LANES:O_Q + (j + 1) * LANES]
            qn = _head_rms_slab(zq, q_gain)
            if rope:
                qn = _rope_slab(qn, cos, sin)
            q_ref[bi, rows, j * LANES:(j + 1) * LANES] = qn.astype(BF16)

        kn = _head_rms_slab(z[:, O_K:O_K + KV_W], kg_ref[...])
        if rope:
            kn = _rope_slab(kn, cos, sin)
        _store_kv(kn, z[:, O_V:O_V + KV_W], k_ref, vt_ref, rows, bi)

        g = _gelu_tanh(z[:, O_G:O_P])
        u_ref[bi, rows, :] = g[:, :SGU_W]
        vn_ref[bi, rows, :] = _rms(g[:, SGU_W:], sg_ref[...]).astype(BF16)

        p_ref[bi, rows, :] = z[:, O_P:]

    starts = [(bi, r0) for bi in range(x_ref.shape[0]) for r0 in range(0, tm, sub)]
    z = project(normalise(starts[0]))
    for cur, nxt in zip(starts, starts[1:] + [None]):
        z_next = project(normalise(nxt)) if nxt is not None else None
        finish(cur, z)
        z = z_next


def _layer_spec(layer, *shape, **kwargs):
    return pl.BlockSpec((None, *shape), lambda *_: (layer,) + (0,) * len(shape), **kwargs)


def _inproj(layer, x, mod, mod_row_of_batch, norm1, w_in, qg, kg, sg, cos, sin, *, rope):
    B, N, _ = x.shape
    tm = min(INPROJ_TILE, N)
    nb = INPROJ_TILE // tm if mod_row_of_batch(0) == mod_row_of_batch(B - 1) else 1
    assert B % nb == 0
    grid = (B // nb, N // tm)
    return pl.pallas_call(
        functools.partial(_inproj_kernel, rope=rope),
        grid=grid,
        in_specs=[
            pl.BlockSpec((nb, tm, D_MODEL), lambda b, i: (b, i, 0)),
            pl.BlockSpec((None, 1, 6, D_MODEL),
                         lambda b, i: (layer, mod_row_of_batch(b * nb), 0, 0)),
            _layer_spec(layer, 1, D_MODEL),
            _layer_spec(layer, D_MODEL, IN_W),
            _layer_spec(layer, 1, LANES),
            _layer_spec(layer, 1, LANES),
            _layer_spec(layer, 1, SGU_W),
            pl.BlockSpec((tm, LANES), lambda b, i: (i, 0)),
            pl.BlockSpec((tm, LANES), lambda b, i: (i, 0)),
        ],
        out_specs=[
            pl.BlockSpec((nb, tm, ATTN_W), lambda b, i: (b, i, 0)),
            pl.BlockSpec((nb, N_KV_HEADS, tm, KV_W), lambda b, i: (b, 0, i, 0)),
            pl.BlockSpec((nb, N_KV_HEADS, 2 * HEAD_DIM, tm), lambda b, i: (b, 0, 0, i)),
            pl.BlockSpec((nb, tm, SGU_W), lambda b, i: (b, i, 0)),
            pl.BlockSpec((nb, tm, SGU_W), lambda b, i: (b, i, 0)),
            pl.BlockSpec((nb, tm, POOL_W), lambda b, i: (b, i, 0)),
        ],
        out_shape=[
            jax.ShapeDtypeStruct((B, N, ATTN_W), BF16),
            jax.ShapeDtypeStruct((B, N_KV_HEADS, N, KV_W), BF16),
            jax.ShapeDtypeStruct((B, N_KV_HEADS, 2 * HEAD_DIM, N), BF16),
            jax.ShapeDtypeStruct((B, N, SGU_W), F32),
            jax.ShapeDtypeStruct((B, N, SGU_W), BF16),
            jax.ShapeDtypeStruct((B, N, POOL_W), F32),
        ],
        compiler_params=pltpu.CompilerParams(
            dimension_semantics=("parallel", "parallel"), vmem_limit_bytes=VMEM_LIMIT),
        name="inproj_rope" if rope else "inproj",
    )(x, mod, norm1, w_in, qg, kg, sg, cos, sin)


def _attn_kernel(*refs, layer, n_kv_sets, n_cast):
    bound_ref, q_ref = refs[:2]
    refs = refs[1:]
    k_refs = refs[1:1 + n_kv_sets]
    vt_refs = refs[1 + n_kv_sets:1 + 2 * n_kv_sets]
    n_in = 1 + 2 * n_kv_sets + n_cast
    o_ref = refs[n_in]
    for src, dst in zip(refs[n_in - n_cast:n_in], refs[n_in + 1:]):
        dst[...] = src[...].astype(BF16)
    tq = q_ref.shape[1]
    lo = _half_mask((tq, LANES))
    zero = jnp.zeros((tq, LANES), BF16)
    nt = (((1,), (1,)), ((), ()))

    def attend(bound):
        def scores(idx):
            slab, half, kvh = idx // 2, idx % 2, idx // Q_PER_KV
            qs = q_ref[0, :, slab * LANES:(slab + 1) * LANES]
            qm = jnp.where(lo, qs, zero) if half == 0 else jnp.where(lo, zero, qs)
            st = [lax.dot_general(k[0, kvh], qm, nt, preferred_element_type=F32) for k in k_refs]
            if bound is not None:
                return st, bound
            return st, functools.reduce(
                jnp.maximum, [jnp.max(si, axis=0, keepdims=True) for si in st])

        def weighted_values(idx, st, shift):
            kvh = idx // Q_PER_KV
            acc = functools.reduce(lambda a, b: a + b, [
                jnp.dot(vt[0, kvh], jnp.exp2(si - shift).astype(BF16), preferred_element_type=F32)
                for si, vt in zip(st, vt_refs)])
            return acc[:HEAD_DIM] / acc[HEAD_DIM:HEAD_DIM + 1]

        heads = []
        pending = scores(0)
        for idx in range(N_HEADS):
            nxt = scores(idx + 1) if idx + 1 < N_HEADS else None
            heads.append(weighted_values(idx, *pending))
            pending = nxt
        o_ref[0] = jnp.concatenate(heads, axis=0).T.astype(BF16)

    bound = bound_ref[layer, 0]
    safe = bound <= MAX_SCORE_BOUND

    @pl.when(safe)
    def _():
        attend(bound)

    @pl.when(jnp.logical_not(safe))
    def _():
        attend(None)


def _attention(layer, score_bounds, q, kv_sets, cast_weights=()):
    B, N, _ = q.shape
    tq = min(Q_TILE, N)
    nq = N // tq
    n_steps = B * nq
    in_specs = [pl.BlockSpec(memory_space=pltpu.SMEM),
                pl.BlockSpec((1, tq, ATTN_W), lambda b, i: (b, i, 0))]
    for k, _ in kv_sets:
        in_specs.append(pl.BlockSpec((1, N_KV_HEADS, k.shape[2], KV_W), lambda b, i: (b, 0, 0, 0)))
    for _, vt in kv_sets:
        in_specs.append(pl.BlockSpec((1, N_KV_HEADS, 2 * HEAD_DIM, vt.shape[3]),
                                     lambda b, i: (b, 0, 0, 0)))
    out_specs = [pl.BlockSpec((1, tq, ATTN_W), lambda b, i: (b, i, 0))]
    out_shape = [jax.ShapeDtypeStruct((B, N, ATTN_W), BF16)]
    for w in cast_weights:
        _, rows, cols = w.shape
        per = next(d for d in (1, 2, 4, 8) if rows % (n_steps // d) == 0
                   and (rows // (n_steps // d)) % BF16_ROWS == 0)
        blk = rows // (n_steps // per)
        in_specs.append(pl.BlockSpec((None, blk, cols),
                                     lambda b, i, per=per: (layer, (b * nq + i) // per, 0)))
        out_specs.append(pl.BlockSpec((blk, cols), lambda b, i, per=per: ((b * nq + i) // per, 0)))
        out_shape.append(jax.ShapeDtypeStruct((rows, cols), BF16))
    outs = pl.pallas_call(
        functools.partial(_attn_kernel, layer=layer, n_kv_sets=len(kv_sets),
                          n_cast=len(cast_weights)),
        grid=(B, nq),
        in_specs=in_specs,
        out_specs=out_specs,
        out_shape=out_shape,
        compiler_params=pltpu.CompilerParams(
            dimension_semantics=("arbitrary", "arbitrary"), vmem_limit_bytes=VMEM_LIMIT),
        name=f"attention_{len(kv_sets)}",
    )(score_bounds, q, *[k for k, _ in kv_sets], *[vt for _, vt in kv_sets], *cast_weights)
    return outs if cast_weights else outs[0]


def _mix_stage(i, n_tiles, x_ref, mod_ref, attn_ref, u_ref, vn_ref, p_ref, pprev_ref, pnext_ref,
               icnt_ref, ws_ref, bs_ref, pw_ref, ps_ref, wo_ref, n2_ref):
    tm = x_ref.shape[1]
    m = mod_ref[0]

    lo = _half_mask((CHUNK, LANES))
    sgu_chunks = []
    for c in range(tm // CHUNK):
        slabs = []
        for j in range(SGU_W // LANES):
            vs = vn_ref[0, c * CHUNK:(c + 1) * CHUNK, j * LANES:(j + 1) * LANES]
            ra = jnp.dot(ws_ref[2 * j], vs, preferred_element_type=F32)
            rb = jnp.dot(ws_ref[2 * j + 1], vs, preferred_element_type=F32)
            slabs.append(jnp.where(lo, ra, rb))
        mixed = jnp.concatenate(slabs, axis=-1) + bs_ref[...]
        sgu_chunks.append((u_ref[0, c * CHUNK:(c + 1) * CHUNK, :] * mixed).astype(BF16))
    sgu = jnp.concatenate(sgu_chunks, axis=0)

    p = p_ref[0]
    prev = jnp.where(i > 0, pprev_ref[0], 0.0)
    nxt = jnp.where(i < n_tiles - 1, pnext_ref[0], 0.0)
    e = jnp.concatenate([prev, p, nxt], axis=0)
    n_ext = tm + 2 * POOL_HALO

    def ahead(a, k):
        return pltpu.roll(a, n_ext - k, axis=0)

    lo = _half_mask((tm, LANES))
    e0, e1 = e[:, :LANES], e[:, LANES:]
    a2 = e0 + ahead(e0, 1)
    a4 = a2 + ahead(a2, 2)
    sums0 = jnp.where(lo, ahead(a2, POOL_HALO - 1)[:tm], ahead(a4, POOL_HALO - 2)[:tm])
    b2 = e1 + ahead(e1, 1)
    b4 = b2 + ahead(b2, 2)
    b8 = b4 + ahead(b4, 4)
    b16 = b8 + ahead(b8, 8)
    sums1 = jnp.where(lo, ahead(b8, POOL_HALO - 4)[:tm], b16[:tm])
    win_sum = jnp.concatenate([sums0, sums1], axis=-1)
    d = win_sum * icnt_ref[...] - p
    pool = jnp.dot(d.astype(BF16), pw_ref[...], preferred_element_type=F32) * ps_ref[...]

    cat = jnp.concatenate([attn_ref[0], sgu, pool.astype(BF16)], axis=-1)
    mix = jnp.dot(cat, wo_ref[...], preferred_element_type=F32)
    x1 = x_ref[0] + m[2:3] * mix
    h2 = (_rms(x1, n2_ref[...] * (1.0 + m[4:5])) + m[3:4]).astype(BF16)
    return x1, h2


def _ffn_stage(x1, h2, gate2, wg_ref, wu_ref, wd_ref, fn_ref, final):
    gate = jnp.dot(h2, wg_ref[...], preferred_element_type=F32)
    up = jnp.dot(h2, wu_ref[...], preferred_element_type=F32)
    act = (_silu(gate) * up).astype(BF16)
    x2 = x1 + gate2 * jnp.dot(act, wd_ref[...], preferred_element_type=F32)
    if final:
        x2 = _rms(x2, fn_ref[...])
    return x2


def _mixffn_kernel(x_ref, mod_ref, modp_ref, attn_ref, u_ref, vn_ref, p_ref, pprev_ref, pnext_ref,
                   icnt_ref, ws_ref, bs_ref, pw_ref, ps_ref, wo_ref, n2_ref, wg_ref, wu_ref, wd_ref,
                   fn_ref, o_ref, x1_scr, h2_scr, *, tiles_per_seq, final):
    s = pl.program_id(0)
    n_steps = pl.num_programs(0)
    i = lax.rem(jnp.minimum(s, n_steps - 2), tiles_per_seq)
    mix_args = (i, tiles_per_seq, x_ref, mod_ref, attn_ref, u_ref, vn_ref, p_ref, pprev_ref,
                pnext_ref, icnt_ref, ws_ref, bs_ref, pw_ref, ps_ref, wo_ref, n2_ref)
    ffn_args = (wg_ref, wu_ref, wd_ref, fn_ref, final)

    @pl.when(s == 0)
    def _():
        x1, h2 = _mix_stage(*mix_args)
        x1_scr[...] = x1
        h2_scr[...] = h2

    @pl.when(jnp.logical_and(s > 0, s < n_steps - 1))
    def _():
        o_ref[0] = _ffn_stage(x1_scr[...], h2_scr[...], modp_ref[0][5:6], *ffn_args)
        x1, h2 = _mix_stage(*mix_args)
        x1_scr[...] = x1
        h2_scr[...] = h2

    @pl.when(s == n_steps - 1)
    def _():
        o_ref[0] = _ffn_stage(x1_scr[...], h2_scr[...], modp_ref[0][5:6], *ffn_args)


def _mixffn(layer, x, mod, mod_row_of_batch, attn, u, vn, p, ws, bs, pw, ps, wo, n2, wg, wu, wd,
            fn, *, final):
    B, N, _ = x.shape
    tm = min(TOKEN_TILE, N)
    tps = N // tm
    n_tiles = B * tps
    hb = tm // POOL_HALO
    n_hblocks = N // POOL_HALO

    def cur(s):
        t = jnp.minimum(s, n_tiles - 1)
        return t // tps, lax.rem(t, tps)

    def prv(s):
        t = jnp.maximum(s - 1, 0)
        return t // tps, lax.rem(t, tps)

    def tile(w):
        return pl.BlockSpec((1, tm, w), lambda s: (*cur(s), 0))

    def halo_prev(s):
        b, i = cur(s)
        return b, jnp.maximum(i * hb - 1, 0), 0

    def halo_next(s):
        b, i = cur(s)
        return b, jnp.minimum((i + 1) * hb, n_hblocks - 1), 0

    once = dict(pipeline_mode=pl.Buffered(1))
    return pl.pallas_call(
        functools.partial(_mixffn_kernel, tiles_per_seq=tps, final=final),
        grid=(n_tiles + 1,),
        in_specs=[
            tile(D_MODEL),
            pl.BlockSpec((None, 1, 6, D_MODEL),
                         lambda s: (layer, mod_row_of_batch(cur(s)[0]), 0, 0)),
            pl.BlockSpec((None, 1, 6, D_MODEL),
                         lambda s: (layer, mod_row_of_batch(prv(s)[0]), 0, 0)),
            tile(ATTN_W), tile(SGU_W), tile(SGU_W), tile(POOL_W),
            pl.BlockSpec((1, POOL_HALO, POOL_W), halo_prev),
            pl.BlockSpec((1, POOL_HALO, POOL_W), halo_next),
            pl.BlockSpec((tm, POOL_W), lambda s: (cur(s)[1], 0)),
            _layer_spec(layer, SGU_GROUPS, CHUNK, CHUNK, **once),
            _layer_spec(layer, CHUNK, SGU_W, **once),
            _layer_spec(layer, POOL_W, POOL_W, **once),
            _layer_spec(layer, 1, POOL_W, **once),
            pl.BlockSpec((MIX_W, D_MODEL), lambda s: (0, 0), **once),
            _layer_spec(layer, 1, D_MODEL, **once),
            pl.BlockSpec((D_MODEL, D_FF), lambda s: (0, 0), **once),
            pl.BlockSpec((D_MODEL, D_FF), lambda s: (0, 0), **once),
            pl.BlockSpec((D_FF, D_MODEL), lambda s: (0, 0), **once),
            pl.BlockSpec((1, D_MODEL), lambda s: (0, 0), **once),
        ],
        out_specs=pl.BlockSpec((1, tm, D_MODEL), lambda s: (*prv(s), 0)),
        out_shape=jax.ShapeDtypeStruct((B, N, D_MODEL), F32),
        scratch_shapes=[pltpu.VMEM((tm, D_MODEL), F32), pltpu.VMEM((tm, D_MODEL), BF16)],
        compiler_params=pltpu.CompilerParams(
            dimension_semantics=("arbitrary",), vmem_limit_bytes=VMEM_LIMIT),
        name="mixffn_final" if final else "mixffn",
    )(x, mod, mod, attn, u, vn, p, p, p, _pool_inv_counts(N), ws, bs, pw, ps, wo, n2, wg, wu, wd,
      fn)


def _rope_tables(n_tokens):
    rows = np.repeat(np.arange(n_tokens // GRID_W, dtype=np.float64), GRID_W)
    cols = np.tile(np.arange(GRID_W, dtype=np.float64), n_tokens // GRID_W)
    inv = ROPE_THETA ** (-np.arange(ROPE_PAIRS_PER_AXIS, dtype=np.float64) / ROPE_PAIRS_PER_AXIS)
    ang = np.concatenate([rows[:, None] * inv, cols[:, None] * inv], axis=-1)
    cos, sin = np.cos(ang), np.sin(ang)
    return (np.tile(cos, (1, 4)).astype(np.float32),
            np.tile(np.concatenate([-sin, sin], axis=-1), (1, 2)).astype(np.float32))


def _pool_inv_counts(n_tokens):
    t = np.arange(n_tokens)[:, None]
    half = np.repeat(np.array(POOL_WINDOWS) // 2, POOL_W // len(POOL_WINDOWS))[None, :]
    cnt = np.minimum(t + half, n_tokens) - np.maximum(t - half, 0)
    return (1.0 / cnt).astype(np.float32)


def _block_diag(w):
    d, g, a, b = w.shape
    eye = jnp.eye(g, dtype=w.dtype)
    return (w[:, :, :, None, :] * eye[None, :, None, :, None]).reshape(d, g * a, g * b)


def kernel(x, c, ctx, c_ctx, w_mod, b_mod, norm1, norm2, w_in, q_norm, k_norm, sgu_norm, w_s, b_s,
           pool_w, pool_scale, w_out, w_gate, w_up, w_down, final_norm):
    B, S, _ = x.shape
    L = ctx.shape[1]
    c_rows = jnp.concatenate(
        [c, c_ctx[None, :], jnp.zeros((MOD_ROWS - B - 1, D_MODEL), F32)], axis=0)
    mod, w_in_bf = _modulation(c_rows, w_mod, b_mod, w_in)
    mod = mod.reshape(DEPTH, MOD_ROWS, 6, D_MODEL)
    cos, sin = _rope_tables(S)
    ones = np.ones((L, LANES), np.float32)

    row = lambda a: a[:, None, :]
    proj_w = (
        row(norm1),
        w_in_bf,
        row(jnp.tile(q_norm, (1, 2))),
        row(jnp.tile(k_norm, (1, 2))),
        row(sgu_norm),
    )
    ws_bf = w_s.astype(BF16)
    sgu_bias = jnp.repeat(jnp.swapaxes(b_s, 1, 2), SGU_W // SGU_GROUPS, axis=2)
    pool_bd = _block_diag(pool_w).astype(BF16)
    lat_row = lambda b: b
    ctx_row = lambda b: B
    bounds = (HEAD_DIM * ATTN_SCALE * LOG2E * BOUND_MARGIN
              * jnp.max(jnp.abs(q_norm), axis=1, keepdims=True)
              * jnp.max(jnp.abs(k_norm), axis=1, keepdims=True))

    for l in range(DEPTH):
        last = l == DEPTH - 1
        q, k, vt, u, vn, p = _inproj(l, x, mod, lat_row, *proj_w, cos, sin, rope=True)
        if last:
            kc, vtc = _kvproj(l, ctx, mod, ctx_row, proj_w[0], proj_w[1], proj_w[3])
        else:
            qc, kc, vtc, uc, vnc, pc = _inproj(l, ctx, mod, ctx_row, *proj_w, ones, ones,
                                               rope=False)
        attn, wo_bf, wg_bf, wu_bf, wd_bf = _attention(
            l, bounds, q, [(kc, vtc), (k, vt)], cast_weights=(w_out, w_gate, w_up, w_down))
        mix_w = (ws_bf, sgu_bias, pool_bd, row(pool_scale), wo_bf, row(norm2), wg_bf, wu_bf, wd_bf,
                 final_norm[None, :])
        if not last:
            attn_c = _attention(l, bounds, qc, [(kc, vtc)])
            ctx = _mixffn(l, ctx, mod, ctx_row, attn_c, uc, vnc, pc, *mix_w, final=False)
        x = _mixffn(l, x, mod, lat_row, attn, u, vn, p, *mix_w, final=last)
    return x
```

```python
import functools
import math

import jax
import jax.numpy as jnp
import numpy as np
from jax import lax
from jax.experimental import pallas as pl
from jax.experimental.pallas import tpu as pltpu

F32 = jnp.float32
BF16 = jnp.bfloat16

D_MODEL = 1024
DEPTH = 2
GRID_W = 64
EPS = 1e-6

HEAD_DIM = 64
N_HEADS = 8
N_KV_HEADS = 2
Q_PER_KV = N_HEADS // N_KV_HEADS
ATTN_W = N_HEADS * HEAD_DIM
KV_W = N_KV_HEADS * HEAD_DIM
ROPE_THETA = 10000.0
ROPE_PAIRS_PER_AXIS = HEAD_DIM // 4
ATTN_SCALE = HEAD_DIM ** -0.5
LOG2E = math.log2(math.e)
BOUND_MARGIN = 1.02
MAX_SCORE_BOUND = 50.0

CHUNK = 128
SGU_W = D_MODEL // 4
SGU_GROUPS = 4
POOL_WINDOWS = (2, 4, 8, 16)
POOL_W = D_MODEL // 4
POOL_HALO = 8

MIX_W = ATTN_W + SGU_W + POOL_W
O_Q = 0
O_K = O_Q + ATTN_W
O_V = O_K + KV_W
O_G = O_V + KV_W
O_P = O_G + 2 * SGU_W
IN_W = O_P + POOL_W
D_FF = 2816

LANES = 128
BF16_ROWS = 16
MOD_ROWS = 16
VMEM_LIMIT = 56 * 1024 * 1024

TOKEN_TILE = 512
INPROJ_TILE = 1024
SUB_TILE = 256
Q_TILE = 512


def _half_mask(shape):
    lane = lax.broadcasted_iota(jnp.int32, shape, len(shape) - 1)
    return (lane & HEAD_DIM) == 0


def _rms(x, gain):
    return x * lax.rsqrt(jnp.mean(x * x, axis=-1, keepdims=True) + EPS) * gain


def _head_rms_slab(z, gain):
    lo = _half_mask(z.shape)
    sq = z * z
    s0 = jnp.sum(jnp.where(lo, sq, 0.0), axis=-1, keepdims=True)
    s1 = jnp.sum(jnp.where(lo, 0.0, sq), axis=-1, keepdims=True)
    r0 = lax.rsqrt(s0 * (1.0 / HEAD_DIM) + EPS)
    r1 = lax.rsqrt(s1 * (1.0 / HEAD_DIM) + EPS)
    return z * jnp.where(lo, r0, r1) * gain


def _rope_slab(x, cos, sin_signed):
    lane = lax.broadcasted_iota(jnp.int32, x.shape, 1)
    first = (lane & (HEAD_DIM // 2)) == 0
    up = pltpu.roll(x, LANES - HEAD_DIM // 2, axis=1)
    down = pltpu.roll(x, HEAD_DIM // 2, axis=1)
    partner = jnp.where(first, up, down)
    return x * cos + partner * sin_signed


def _gelu_tanh(x):
    c = math.sqrt(2.0 / math.pi)
    k0 = -2.0 * c * LOG2E
    k1 = k0 * 0.044715
    return x / (1.0 + jnp.exp2(x * (k1 * (x * x) + k0)))


def _silu(x):
    return x / (1.0 + jnp.exp(-x))


def _mod_kernel(c_ref, w_ref, b_ref, win_ref, o_ref, win_bf_ref):
    win_bf_ref[...] = win_ref[...].astype(BF16)

    a = _silu(c_ref[...])
    w = w_ref[0]
    a_hi = a.astype(BF16)
    a_lo = (a - a_hi.astype(F32)).astype(BF16)
    w_hi = w.astype(BF16)
    w_lo = (w - w_hi.astype(F32)).astype(BF16)
    dot = functools.partial(jnp.dot, preferred_element_type=F32)
    o_ref[0] = dot(a_hi, w_hi) + (dot(a_lo, w_hi) + dot(a_hi, w_lo)) + b_ref[0]


def _modulation(c_rows, w_mod, b_mod, w_in):
    n_col = 6
    win_cols = IN_W // n_col
    win_spec = pl.BlockSpec((None, D_MODEL, win_cols), lambda l, j: (l, 0, j))
    return pl.pallas_call(
        _mod_kernel,
        grid=(DEPTH, n_col),
        in_specs=[
            pl.BlockSpec((MOD_ROWS, D_MODEL), lambda l, j: (0, 0)),
            pl.BlockSpec((1, D_MODEL, D_MODEL), lambda l, j: (l, 0, j)),
            pl.BlockSpec((1, 1, D_MODEL), lambda l, j: (l, 0, j)),
            win_spec,
        ],
        out_specs=[pl.BlockSpec((1, MOD_ROWS, D_MODEL), lambda l, j: (l, 0, j)), win_spec],
        out_shape=[jax.ShapeDtypeStruct((DEPTH, MOD_ROWS, 6 * D_MODEL), F32),
                   jax.ShapeDtypeStruct(w_in.shape, BF16)],
        compiler_params=pltpu.CompilerParams(dimension_semantics=("parallel", "parallel")),
        name="modulation",
    )(c_rows, w_mod, b_mod.reshape(DEPTH, 1, 6 * D_MODEL), w_in)


def _store_kv(kn, v, k_ref, vt_ref, rows):
    k_swapped = pltpu.roll(kn, HEAD_DIM, axis=1)
    lo = _half_mask(kn.shape)
    k_ref[0, 0, rows, :] = jnp.where(lo, kn, k_swapped).astype(BF16)
    k_ref[0, 1, rows, :] = jnp.where(lo, k_swapped, kn).astype(BF16)
    vt = v.T
    ones = jnp.ones((HEAD_DIM, vt.shape[1]), F32)
    for hh in range(N_KV_HEADS):
        vt_ref[0, hh, :, rows] = jnp.concatenate(
            [vt[hh * HEAD_DIM:(hh + 1) * HEAD_DIM], ones], axis=0).astype(BF16)


def _kvproj_kernel(x_ref, mod_ref, n1_ref, w_ref, kg_ref, k_ref, vt_ref):
    m = mod_ref[0]
    h = _rms(x_ref[0], n1_ref[...] * (1.0 + m[1:2])) + m[0:1]
    z = jnp.dot(h.astype(BF16), w_ref[...], preferred_element_type=F32)
    kn = _head_rms_slab(z[:, :KV_W], kg_ref[...])
    _store_kv(kn, z[:, KV_W:], k_ref, vt_ref, slice(None))


def _kvproj(layer, x, mod, mod_row_of_batch, norm1, w_in, kg):
    B, N, _ = x.shape
    tm = min(INPROJ_TILE, N)
    assert O_K % (2 * KV_W) == 0 and O_V == O_K + KV_W
    return pl.pallas_call(
        _kvproj_kernel,
        grid=(B, N // tm),
        in_specs=[
            pl.BlockSpec((1, tm, D_MODEL), lambda b, i: (b, i, 0)),
            pl.BlockSpec((None, 1, 6, D_MODEL), lambda b, i: (layer, mod_row_of_batch(b), 0, 0)),
            _layer_spec(layer, 1, D_MODEL),
            pl.BlockSpec((None, D_MODEL, 2 * KV_W), lambda b, i: (layer, 0, O_K // (2 * KV_W))),
            _layer_spec(layer, 1, LANES),
        ],
        out_specs=[
            pl.BlockSpec((1, N_KV_HEADS, tm, KV_W), lambda b, i: (b, 0, i, 0)),
            pl.BlockSpec((1, N_KV_HEADS, 2 * HEAD_DIM, tm), lambda b, i: (b, 0, 0, i)),
        ],
        out_shape=[
            jax.ShapeDtypeStruct((B, N_KV_HEADS, N, KV_W), BF16),
            jax.ShapeDtypeStruct((B, N_KV_HEADS, 2 * HEAD_DIM, N), BF16),
        ],
        compiler_params=pltpu.CompilerParams(
            dimension_semantics=("parallel", "parallel"), vmem_limit_bytes=VMEM_LIMIT),
        name="kvproj",
    )(x, mod, norm1, w_in, kg)


def _inproj_kernel(x_ref, mod_ref, n1_ref, w_ref, qg_ref, kg_ref, sg_ref, cos_ref, sin_ref,
                   q_ref, k_ref, vt_ref, u_ref, vn_ref, p_ref, *, rope):
    m = mod_ref[0]
    tm = x_ref.shape[1]
    sub = min(SUB_TILE, tm)

    gain1 = n1_ref[...] * (1.0 + m[1:2])
    q_gain = qg_ref[...] * (ATTN_SCALE * LOG2E)

    def normalise(r0):
        return (_rms(x_ref[0, r0:r0 + sub, :], gain1) + m[0:1]).astype(BF16)

    def project(h):
        return jnp.dot(h, w_ref[...], preferred_element_type=F32)

    def finish(r0, z):
        rows = slice(r0, r0 + sub)
        if rope:
            cos = cos_ref[rows, :]
            sin = sin_ref[rows, :]

        for j in range(ATTN_W // LANES):
            zq = z[:, O_Q + j * LANES:O_Q + (j + 1) * LANES]
            qn = _head_rms_slab(zq, q_gain)
            if rope:
                qn = _rope_slab(qn, cos, sin)
            q_ref[0, rows, j * LANES:(j + 1) * LANES] = qn.astype(BF16)

        kn = _head_rms_slab(z[:, O_K:O_K + KV_W], kg_ref[...])
        if rope:
            kn = _rope_slab(kn, cos, sin)
        _store_kv(kn, z[:, O_V:O_V + KV_W], k_ref, vt_ref, rows)

        g = _gelu_tanh(z[:, O_G:O_P])
        u_ref[0, rows, :] = g[:, :SGU_W]
        vn_ref[0, rows, :] = _rms(g[:, SGU_W:], sg_ref[...]).astype(BF16)

        p_ref[0, rows, :] = z[:, O_P:]

    starts = list(range(0, tm, sub))
    z = project(normalise(starts[0]))
    for r0, r1 in zip(starts, starts[1:] + [None]):
        z_next = project(normalise(r1)) if r1 is not None else None
        finish(r0, z)
        z = z_next


def _layer_spec(layer, *shape, **kwargs):
    return pl.BlockSpec((None, *shape), lambda *_: (layer,) + (0,) * len(shape), **kwargs)


def _inproj(layer, x, mod, mod_row_of_batch, norm1, w_in, qg, kg, sg, cos, sin, *, rope):
    B, N, _ = x.shape
    tm = min(INPROJ_TILE, N)
    grid = (B, N // tm)
    return pl.pallas_call(
        functools.partial(_inproj_kernel, rope=rope),
        grid=grid,
        in_specs=[
            pl.BlockSpec((1, tm, D_MODEL), lambda b, i: (b, i, 0)),
            pl.BlockSpec((None, 1, 6, D_MODEL), lambda b, i: (layer, mod_row_of_batch(b), 0, 0)),
            _layer_spec(layer, 1, D_MODEL),
            _layer_spec(layer, D_MODEL, IN_W),
            _layer_spec(layer, 1, LANES),
            _layer_spec(layer, 1, LANES),
            _layer_spec(layer, 1, SGU_W),
            pl.BlockSpec((tm, LANES), lambda b, i: (i, 0)),
            pl.BlockSpec((tm, LANES), lambda b, i: (i, 0)),
        ],
        out_specs=[
            pl.BlockSpec((1, tm, ATTN_W), lambda b, i: (b, i, 0)),
            pl.BlockSpec((1, N_KV_HEADS, tm, KV_W), lambda b, i: (b, 0, i, 0)),
            pl.BlockSpec((1, N_KV_HEADS, 2 * HEAD_DIM, tm), lambda b, i: (b, 0, 0, i)),
            pl.BlockSpec((1, tm, SGU_W), lambda b, i: (b, i, 0)),
            pl.BlockSpec((1, tm, SGU_W), lambda b, i: (b, i, 0)),
            pl.BlockSpec((1, tm, POOL_W), lambda b, i: (b, i, 0)),
        ],
        out_shape=[
            jax.ShapeDtypeStruct((B, N, ATTN_W), BF16),
            jax.ShapeDtypeStruct((B, N_KV_HEADS, N, KV_W), BF16),
            jax.ShapeDtypeStruct((B, N_KV_HEADS, 2 * HEAD_DIM, N), BF16),
            jax.ShapeDtypeStruct((B, N, SGU_W), F32),
            jax.ShapeDtypeStruct((B, N, SGU_W), BF16),
            jax.ShapeDtypeStruct((B, N, POOL_W), F32),
        ],
        compiler_params=pltpu.CompilerParams(
            dimension_semantics=("parallel", "parallel"), vmem_limit_bytes=VMEM_LIMIT),
        name="inproj_rope" if rope else "inproj",
    )(x, mod, norm1, w_in, qg, kg, sg, cos, sin)


def _attn_kernel(*refs, layer, n_kv_sets, n_cast):
    bound_ref, q_ref = refs[:2]
    refs = refs[1:]
    k_refs = refs[1:1 + n_kv_sets]
    vt_refs = refs[1 + n_kv_sets:1 + 2 * n_kv_sets]
    n_in = 1 + 2 * n_kv_sets + n_cast
    o_ref = refs[n_in]
    for src, dst in zip(refs[n_in - n_cast:n_in], refs[n_in + 1:]):
        dst[...] = src[...].astype(BF16)
    tq = q_ref.shape[1]
    lo = _half_mask((tq, LANES))
    zero = jnp.zeros((tq, LANES), BF16)
    nt = (((1,), (1,)), ((), ()))

    def attend(bound):
        def scores(idx):
            slab, half, kvh = idx // 2, idx % 2, idx // Q_PER_KV
            qs = q_ref[0, :, slab * LANES:(slab + 1) * LANES]
            qm = jnp.where(lo, qs, zero) if half == 0 else jnp.where(lo, zero, qs)
            st = [lax.dot_general(k[0, kvh], qm, nt, preferred_element_type=F32) for k in k_refs]
            if bound is not None:
                return st, bound
            return st, functools.reduce(
                jnp.maximum, [jnp.max(si, axis=0, keepdims=True) for si in st])

        def weighted_values(idx, st, shift):
            kvh = idx // Q_PER_KV
            acc = functools.reduce(lambda a, b: a + b, [
                jnp.dot(vt[0, kvh], jnp.exp2(si - shift).astype(BF16), preferred_element_type=F32)
                for si, vt in zip(st, vt_refs)])
            return acc[:HEAD_DIM] / acc[HEAD_DIM:HEAD_DIM + 1]

        heads = []
        pending = scores(0)
        for idx in range(N_HEADS):
            nxt = scores(idx + 1) if idx + 1 < N_HEADS else None
            heads.append(weighted_values(idx, *pending))
            pending = nxt
        o_ref[0] = jnp.concatenate(heads, axis=0).T.astype(BF16)

    bound = bound_ref[layer, 0]
    safe = bound <= MAX_SCORE_BOUND

    @pl.when(safe)
    def _():
        attend(bound)

    @pl.when(jnp.logical_not(safe))
    def _():
        attend(None)


def _attention(layer, score_bounds, q, kv_sets, cast_weights=()):
    B, N, _ = q.shape
    tq = min(Q_TILE, N)
    nq = N // tq
    n_steps = B * nq
    in_specs = [pl.BlockSpec(memory_space=pltpu.SMEM),
                pl.BlockSpec((1, tq, ATTN_W), lambda b, i: (b, i, 0))]
    for k, _ in kv_sets:
        in_specs.append(pl.BlockSpec((1, N_KV_HEADS, k.shape[2], KV_W), lambda b, i: (b, 0, 0, 0),
                                     pipeline_mode=pl.Buffered(1)))
    for _, vt in kv_sets:
        in_specs.append(pl.BlockSpec((1, N_KV_HEADS, 2 * HEAD_DIM, vt.shape[3]),
                                     lambda b, i: (b, 0, 0, 0), pipeline_mode=pl.Buffered(1)))
    out_specs = [pl.BlockSpec((1, tq, ATTN_W), lambda b, i: (b, i, 0))]
    out_shape = [jax.ShapeDtypeStruct((B, N, ATTN_W), BF16)]
    for w in cast_weights:
        _, rows, cols = w.shape
        per = next(d for d in (1, 2, 4, 8) if rows % (n_steps // d) == 0
                   and (rows // (n_steps // d)) % BF16_ROWS == 0)
        blk = rows // (n_steps // per)
        in_specs.append(pl.BlockSpec((None, blk, cols),
                                     lambda b, i, per=per: (layer, (b * nq + i) // per, 0)))
        out_specs.append(pl.BlockSpec((blk, cols), lambda b, i, per=per: ((b * nq + i) // per, 0)))
        out_shape.append(jax.ShapeDtypeStruct((rows, cols), BF16))
    outs = pl.pallas_call(
        functools.partial(_attn_kernel, layer=layer, n_kv_sets=len(kv_sets),
                          n_cast=len(cast_weights)),
        grid=(B, nq),
        in_specs=in_specs,
        out_specs=out_specs,
        out_shape=out_shape,
        compiler_params=pltpu.CompilerParams(
            dimension_semantics=("arbitrary", "arbitrary"), vmem_limit_bytes=VMEM_LIMIT),
        name=f"attention_{len(kv_sets)}",
    )(score_bounds, q, *[k for k, _ in kv_sets], *[vt for _, vt in kv_sets], *cast_weights)
    return outs if cast_weights else outs[0]


def _mix_stage(i, n_tiles, x_ref, mod_ref, attn_ref, u_ref, vn_ref, p_ref, pprev_ref, pnext_ref,
               icnt_ref, ws_ref, bs_ref, pw_ref, ps_ref, wo_ref, n2_ref):
    tm = x_ref.shape[1]
    m = mod_ref[0]

    lo = _half_mask((CHUNK, LANES))
    sgu_chunks = []
    for c in range(tm // CHUNK):
        slabs = []
        for j in range(SGU_W // LANES):
            vs = vn_ref[0, c * CHUNK:(c + 1) * CHUNK, j * LANES:(j + 1) * LANES]
            ra = jnp.dot(ws_ref[2 * j], vs, preferred_element_type=F32)
            rb = jnp.dot(ws_ref[2 * j + 1], vs, preferred_element_type=F32)
            slabs.append(jnp.where(lo, ra, rb))
        mixed = jnp.concatenate(slabs, axis=-1) + bs_ref[...]
        sgu_chunks.append((u_ref[0, c * CHUNK:(c + 1) * CHUNK, :] * mixed).astype(BF16))
    sgu = jnp.concatenate(sgu_chunks, axis=0)

    p = p_ref[0]
    prev = jnp.where(i > 0, pprev_ref[0], 0.0)
    nxt = jnp.where(i < n_tiles - 1, pnext_ref[0], 0.0)
    e = jnp.concatenate([prev, p, nxt], axis=0)
    n_ext = tm + 2 * POOL_HALO

    def ahead(a, k):
        return pltpu.roll(a, n_ext - k, axis=0)

    lo = _half_mask((tm, LANES))
    e0, e1 = e[:, :LANES], e[:, LANES:]
    a2 = e0 + ahead(e0, 1)
    a4 = a2 + ahead(a2, 2)
    sums0 = jnp.where(lo, ahead(a2, POOL_HALO - 1)[:tm], ahead(a4, POOL_HALO - 2)[:tm])
    b2 = e1 + ahead(e1, 1)
    b4 = b2 + ahead(b2, 2)
    b8 = b4 + ahead(b4, 4)
    b16 = b8 + ahead(b8, 8)
    sums1 = jnp.where(lo, ahead(b8, POOL_HALO - 4)[:tm], b16[:tm])
    win_sum = jnp.concatenate([sums0, sums1], axis=-1)
    d = win_sum * icnt_ref[...] - p
    pool = jnp.dot(d.astype(BF16), pw_ref[...], preferred_element_type=F32) * ps_ref[...]

    cat = jnp.concatenate([attn_ref[0], sgu, pool.astype(BF16)], axis=-1)
    mix = jnp.dot(cat, wo_ref[...], preferred_element_type=F32)
    x1 = x_ref[0] + m[2:3] * mix
    h2 = (_rms(x1, n2_ref[...] * (1.0 + m[4:5])) + m[3:4]).astype(BF16)
    return x1, h2


def _ffn_stage(x1, h2, gate2, wg_ref, wu_ref, wd_ref, fn_ref, final):
    gate = jnp.dot(h2, wg_ref[...], preferred_element_type=F32)
    up = jnp.dot(h2, wu_ref[...], preferred_element_type=F32)
    act = (_silu(gate) * up).astype(BF16)
    x2 = x1 + gate2 * jnp.dot(act, wd_ref[...], preferred_element_type=F32)
    if final:
        x2 = _rms(x2, fn_ref[...])
    return x2


def _mixffn_kernel(x_ref, mod_ref, modp_ref, attn_ref, u_ref, vn_ref, p_ref, pprev_ref, pnext_ref,
                   icnt_ref, ws_ref, bs_ref, pw_ref, ps_ref, wo_ref, n2_ref, wg_ref, wu_ref, wd_ref,
                   fn_ref, o_ref, x1_scr, h2_scr, *, tiles_per_seq, final):
    s = pl.program_id(0)
    n_steps = pl.num_programs(0)
    i = lax.rem(jnp.minimum(s, n_steps - 2), tiles_per_seq)
    mix_args = (i, tiles_per_seq, x_ref, mod_ref, attn_ref, u_ref, vn_ref, p_ref, pprev_ref,
                pnext_ref, icnt_ref, ws_ref, bs_ref, pw_ref, ps_ref, wo_ref, n2_ref)
    ffn_args = (wg_ref, wu_ref, wd_ref, fn_ref, final)

    @pl.when(s == 0)
    def _():
        x1, h2 = _mix_stage(*mix_args)
        x1_scr[...] = x1
        h2_scr[...] = h2

    @pl.when(jnp.logical_and(s > 0, s < n_steps - 1))
    def _():
        o_ref[0] = _ffn_stage(x1_scr[...], h2_scr[...], modp_ref[0][5:6], *ffn_args)
        x1, h2 = _mix_stage(*mix_args)
        x1_scr[...] = x1
        h2_scr[...] = h2

    @pl.when(s == n_steps - 1)
    def _():
        o_ref[0] = _ffn_stage(x1_scr[...], h2_scr[...], modp_ref[0][5:6], *ffn_args)


def _mixffn(layer, x, mod, mod_row_of_batch, attn, u, vn, p, ws, bs, pw, ps, wo, n2, wg, wu, wd,
            fn, *, final):
    B, N, _ = x.shape
    tm = min(TOKEN_TILE, N)
    tps = N // tm
    n_tiles = B * tps
    hb = tm // POOL_HALO
    n_hblocks = N // POOL_HALO

    def cur(s):
        t = jnp.minimum(s, n_tiles - 1)
        return t // tps, lax.rem(t, tps)

    def prv(s):
        t = jnp.maximum(s - 1, 0)
        return t // tps, lax.rem(t, tps)

    def tile(w):
        return pl.BlockSpec((1, tm, w), lambda s: (*cur(s), 0))

    def halo_prev(s):
        b, i = cur(s)
        return b, jnp.maximum(i * hb - 1, 0), 0

    def halo_next(s):
        b, i = cur(s)
        return b, jnp.minimum((i + 1) * hb, n_hblocks - 1), 0

    once = dict(pipeline_mode=pl.Buffered(1))
    return pl.pallas_call(
        functools.partial(_mixffn_kernel, tiles_per_seq=tps, final=final),
        grid=(n_tiles + 1,),
        in_specs=[
            tile(D_MODEL),
            pl.BlockSpec((None, 1, 6, D_MODEL),
                         lambda s: (layer, mod_row_of_batch(cur(s)[0]), 0, 0)),
            pl.BlockSpec((None, 1, 6, D_MODEL),
                         lambda s: (layer, mod_row_of_batch(prv(s)[0]), 0, 0)),
            tile(ATTN_W), tile(SGU_W), tile(SGU_W), tile(POOL_W),
            pl.BlockSpec((1, POOL_HALO, POOL_W), halo_prev),
            pl.BlockSpec((1, POOL_HALO, POOL_W), halo_next),
            pl.BlockSpec((tm, POOL_W), lambda s: (cur(s)[1], 0)),
            _layer_spec(layer, SGU_GROUPS, CHUNK, CHUNK, **once),
            _layer_spec(layer, CHUNK, SGU_W, **once),
            _layer_spec(layer, POOL_W, POOL_W, **once),
            _layer_spec(layer, 1, POOL_W, **once),
            pl.BlockSpec((MIX_W, D_MODEL), lambda s: (0, 0), **once),
            _layer_spec(layer, 1, D_MODEL, **once),
            pl.BlockSpec((D_MODEL, D_FF), lambda s: (0, 0), **once),
            pl.BlockSpec((D_MODEL, D_FF), lambda s: (0, 0), **once),
            pl.BlockSpec((D_FF, D_MODEL), lambda s: (0, 0), **once),
            pl.BlockSpec((1, D_MODEL), lambda s: (0, 0), **once),
        ],
        out_specs=pl.BlockSpec((1, tm, D_MODEL), lambda s: (*prv(s), 0)),
        out_shape=jax.ShapeDtypeStruct((B, N, D_MODEL), F32),
        scratch_shapes=[pltpu.VMEM((tm, D_MODEL), F32), pltpu.VMEM((tm, D_MODEL), BF16)],
        compiler_params=pltpu.CompilerParams(
            dimension_semantics=("arbitrary",), vmem_limit_bytes=VMEM_LIMIT),
        name="mixffn_final" if final else "mixffn",
    )(x, mod, mod, attn, u, vn, p, p, p, _pool_inv_counts(N), ws, bs, pw, ps, wo, n2, wg, wu, wd,
      fn)


def _rope_tables(n_tokens):
    rows = np.repeat(np.arange(n_tokens // GRID_W, dtype=np.float64), GRID_W)
    cols = np.tile(np.arange(GRID_W, dtype=np.float64), n_tokens // GRID_W)
    inv = ROPE_THETA ** (-np.arange(ROPE_PAIRS_PER_AXIS, dtype=np.float64) / ROPE_PAIRS_PER_AXIS)
    ang = np.concatenate([rows[:, None] * inv, cols[:, None] * inv], axis=-1)
    cos, sin = np.cos(ang), np.sin(ang)
    return (np.tile(cos, (1, 4)).astype(np.float32),
            np.tile(np.concatenate([-sin, sin], axis=-1), (1, 2)).astype(np.float32))


def _pool_inv_counts(n_tokens):
    t = np.arange(n_tokens)[:, None]
    half = np.repeat(np.array(POOL_WINDOWS) // 2, POOL_W // len(POOL_WINDOWS))[None, :]
    cnt = np.minimum(t + half, n_tokens) - np.maximum(t - half, 0)
    return (1.0 / cnt).astype(np.float32)


def _block_diag(w):
    d, g, a, b = w.shape
    eye = jnp.eye(g, dtype=w.dtype)
    return (w[:, :, :, None, :] * eye[None, :, None, :, None]).reshape(d, g * a, g * b)


def kernel(x, c, ctx, c_ctx, w_mod, b_mod, norm1, norm2, w_in, q_norm, k_norm, sgu_norm, w_s, b_s,
           pool_w, pool_scale, w_out, w_gate, w_up, w_down, final_norm):
    B, S, _ = x.shape
    L = ctx.shape[1]
    c_rows = jnp.concatenate(
        [c, c_ctx[None, :], jnp.zeros((MOD_ROWS - B - 1, D_MODEL), F32)], axis=0)
    mod, w_in_bf = _modulation(c_rows, w_mod, b_mod, w_in)
    mod = mod.reshape(DEPTH, MOD_ROWS, 6, D_MODEL)
    cos, sin = _rope_tables(S)
    ones = np.ones((L, LANES), np.float32)

    row = lambda a: a[:, None, :]
    proj_w = (
        row(norm1),
        w_in_bf,
        row(jnp.tile(q_norm, (1, 2))),
        row(jnp.tile(k_norm, (1, 2))),
        row(sgu_norm),
    )
    ws_bf = w_s.astype(BF16)
    sgu_bias = jnp.repeat(jnp.swapaxes(b_s, 1, 2), SGU_W // SGU_GROUPS, axis=2)
    pool_bd = _block_diag(pool_w).astype(BF16)
    lat_row = lambda b: b
    ctx_row = lambda b: B
    bounds = (HEAD_DIM * ATTN_SCALE * LOG2E * BOUND_MARGIN
              * jnp.max(jnp.abs(q_norm), axis=1, keepdims=True)
              * jnp.max(jnp.abs(k_norm), axis=1, keepdims=True))

    for l in range(DEPTH):
        last = l == DEPTH - 1
        q, k, vt, u, vn, p = _inproj(l, x, mod, lat_row, *proj_w, cos, sin, rope=True)
        if last:
            kc, vtc = _kvproj(l, ctx, mod, ctx_row, proj_w[0], proj_w[1], proj_w[3])
        else:
            qc, kc, vtc, uc, vnc, pc = _inproj(l, ctx, mod, ctx_row, *proj_w, ones, ones,
                                               rope=False)
        attn, wo_bf, wg_bf, wu_bf, wd_bf = _attention(
            l, bounds, q, [(kc, vtc), (k, vt)], cast_weights=(w_out, w_gate, w_up, w_down))
        mix_w = (ws_bf, sgu_bias, pool_bd, row(pool_scale), wo_bf, row(norm2), wg_bf, wu_bf, wd_bf,
                 final_norm[None, :])
        if not last:
            attn_c = _attention(l, bounds, qc, [(kc, vtc)])
            ctx = _mixffn(l, ctx, mod, ctx_row, attn_c, uc, vnc, pc, *mix_w, final=False)
        x = _mixffn(l, x, mod, lat_row, attn, u, vn, p, *mix_w, final=last)
    return x
```
